```python
import math
import jax
import jax.numpy as jnp
from jax import lax
import numpy as np

D_MODEL = 1024
BATCH = 2
SEQ = 16384
DEPTH = 2

A_HEADS = 8
A_HEAD_DIM = 64
IDX_HEADS = 8
IDX_DIM = 64
TOPK_MAX = 256
B_HEADS = 8
Q_RANK = 256
KV_RANK = 128
NOPE_DIM = 64
ROPE_DIM = 32
V_DIM = 64
ROPE_THETA = 10000.0
NUM_BUCKETS = 32
MAX_DISTANCE = 128
D_FF = 2816
CONV_WIDTH = 3
Q_BLOCK = 128
LN_EPS = 1e-5
RMS_EPS = 1e-6
DEEPNORM_ALPHA = (2 * DEPTH) ** 0.25
DEEPNORM_BETA = (8 * DEPTH) ** -0.25
A_WIDTH = A_HEADS * A_HEAD_DIM
B_WIDTH = B_HEADS * V_DIM
SPLIT_SIZES = (A_WIDTH, A_WIDTH, A_WIDTH, IDX_HEADS * IDX_DIM, IDX_DIM, IDX_HEADS,
               Q_RANK, KV_RANK, ROPE_DIM, D_MODEL, D_MODEL)
IN_WIDTH = sum(SPLIT_SIZES)

kernel_name = "hybrid_dsa_mla_convffn_deepnorm"


def layer_norm(x, g, b):
    xf = x.astype(jnp.float32)
    mu = jnp.mean(xf, axis=-1, keepdims=True)
    var = jnp.mean(jnp.square(xf - mu), axis=-1, keepdims=True)
    return ((xf - mu) * lax.rsqrt(var + LN_EPS)).astype(x.dtype) * g + b


def rms_norm(x, g):
    xf = x.astype(jnp.float32)
    return (xf * lax.rsqrt(jnp.mean(xf * xf, axis=-1, keepdims=True) + RMS_EPS)).astype(x.dtype) * g


def apply_rope(x, positions):
    half = ROPE_DIM // 2
    inv_freq = ROPE_THETA ** (-jnp.arange(half, dtype=jnp.float32) * (2.0 / ROPE_DIM))
    ang = positions.astype(jnp.float32)[:, :, None] * inv_freq
    ang = ang.reshape(ang.shape[:2] + (1,) * (x.ndim - 3) + (half,))
    cos, sin = jnp.cos(ang), jnp.sin(ang)
    x1 = x[..., :half].astype(jnp.float32)
    x2 = x[..., half:].astype(jnp.float32)
    return jnp.concatenate([x1 * cos - x2 * sin, x1 * sin + x2 * cos], axis=-1).astype(x.dtype)


def t5_bucket(rel):
    n = jnp.maximum(rel, 0)
    max_exact = NUM_BUCKETS // 2
    log_ratio = jnp.log(jnp.maximum(n, 1).astype(jnp.float32) / max_exact) / math.log(MAX_DISTANCE / max_exact)
    large = max_exact + (log_ratio * (NUM_BUCKETS - max_exact)).astype(jnp.int32)
    large = jnp.minimum(large, NUM_BUCKETS - 1)
    return jnp.where(n < max_exact, n, large)


def dsa_attention(qa, ka, va, iq, ik, iw, positions, rel_bias):
    bsz, seq = qa.shape[0], qa.shape[1]
    topk = min(TOPK_MAX, seq // 4)
    n_blocks = seq // Q_BLOCK
    key_idx = jnp.arange(seq, dtype=jnp.int32)
    a_scale = A_HEAD_DIM ** -0.5
    i_scale = IDX_DIM ** -0.5
    gather = jax.vmap(lambda t, ii: t[ii])

    def block(i):
        start = i * Q_BLOCK
        q = lax.dynamic_slice_in_dim(qa, start, Q_BLOCK, axis=1)
        q_i = lax.dynamic_slice_in_dim(iq, start, Q_BLOCK, axis=1)
        w_i = lax.dynamic_slice_in_dim(iw, start, Q_BLOCK, axis=1)
        q_pos = lax.dynamic_slice_in_dim(positions, start, Q_BLOCK, axis=1)
        q_idx = start + jnp.arange(Q_BLOCK, dtype=jnp.int32)
        causal = key_idx[None, :] <= q_idx[:, None]
        head_scores = jax.nn.relu(jnp.einsum('bqhd,bsd->bqhs', q_i, ik) * i_scale)
        scores = jnp.einsum('bqhs,bqh->bqs', head_scores, w_i).astype(jnp.float32)
        scores = jnp.where(causal[None], scores, -jnp.inf)
        _, sel = lax.top_k(scores, topk)
        valid = sel <= q_idx[None, :, None]
        k_sel = gather(ka, sel)
        v_sel = gather(va, sel)
        k_pos = gather(positions, sel)
        bias = rel_bias[t5_bucket(q_pos[:, :, None] - k_pos)]
        logits = (jnp.einsum('bqhd,bqkhd->bqkh', q, k_sel).astype(jnp.float32) * a_scale
                  + bias.astype(jnp.float32))
        logits = jnp.where(valid[..., None], logits, -jnp.inf)
        p = jax.nn.softmax(logits, axis=2).astype(va.dtype)
        return jnp.einsum('bqkh,bqkhd->bqhd', p, v_sel)

    out = lax.map(block, jnp.arange(n_blocks, dtype=jnp.int32))
    return out.transpose(1, 0, 2, 3, 4).reshape(bsz, seq, A_WIDTH)


def mla_attention(q_nope, q_rope, k_nope, k_rope, v):
    bsz, seq = q_nope.shape[0], q_nope.shape[1]
    n_blocks = seq // Q_BLOCK
    key_idx = jnp.arange(seq, dtype=jnp.int32)
    scale = (NOPE_DIM + ROPE_DIM) ** -0.5

    def block(i):
        start = i * Q_BLOCK
        qn = lax.dynamic_slice_in_dim(q_nope, start, Q_BLOCK, axis=1)
        qr = lax.dynamic_slice_in_dim(q_rope, start, Q_BLOCK, axis=1)
        q_idx = start + jnp.arange(Q_BLOCK, dtype=jnp.int32)
        causal = key_idx[None, :] <= q_idx[:, None]
        logits = (jnp.einsum('bqhd,bshd->bhqs', qn, k_nope)
                  + jnp.einsum('bqhr,bsr->bhqs', qr, k_rope)).astype(jnp.float32) * scale
        logits = jnp.where(causal[None, None], logits, -jnp.inf)
        p = jax.nn.softmax(logits, axis=-1).astype(v.dtype)
        return jnp.einsum('bhqs,bshd->bqhd', p, v)

    out = lax.map(block, jnp.arange(n_blocks, dtype=jnp.int32))
    return out.transpose(1, 0, 2, 3, 4).reshape(bsz, seq, B_WIDTH)


def token_mixer(h, positions, rel_bias, w_in, q_norm_g, w_uq, kv_norm_g, w_ukv,
                w_branch_a, w_branch_b, w_out):
    bsz, seq = h.shape[0], h.shape[1]
    proj = h @ w_in
    split_at = np.cumsum(SPLIT_SIZES)[:-1].tolist()
    qa, ka, va, iq, ik, iw, cq, ckv, kr, ga, gb = jnp.split(proj, split_at, axis=-1)
    heads_a = (bsz, seq, A_HEADS, A_HEAD_DIM)
    ya = dsa_attention(qa.reshape(heads_a), ka.reshape(heads_a), va.reshape(heads_a),
                       iq.reshape(bsz, seq, IDX_HEADS, IDX_DIM), ik,
                       iw * (IDX_HEADS ** -0.5), positions, rel_bias)
    q = (rms_norm(cq, q_norm_g) @ w_uq).reshape(bsz, seq, B_HEADS, NOPE_DIM + ROPE_DIM)
    q_nope, q_rope = q[..., :NOPE_DIM], apply_rope(q[..., NOPE_DIM:], positions)
    kv = (rms_norm(ckv, kv_norm_g) @ w_ukv).reshape(bsz, seq, B_HEADS, NOPE_DIM + V_DIM)
    k_nope, v = kv[..., :NOPE_DIM], kv[..., NOPE_DIM:]
    k_rope = apply_rope(kr, positions)
    yb = mla_attention(q_nope, q_rope, k_nope, k_rope, v)
    merged = jax.nn.sigmoid(ga) * (ya @ w_branch_a) + jax.nn.sigmoid(gb) * (yb @ w_branch_b)
    return merged @ w_out


def conv_ffn(h, w_up, conv_w, conv_b, w_down):
    u = h @ w_up
    ch = u.shape[-1]
    u = lax.conv_general_dilated(u, conv_w[:, None, :], window_strides=(1,),
                                 padding=[(CONV_WIDTH - 1, 0)],
                                 dimension_numbers=('NWC', 'WIO', 'NWC'),
                                 feature_group_count=ch) + conv_b
    g, val = jnp.split(u, 2, axis=-1)
    return (jax.nn.silu(g) * val) @ w_down


def setup_inputs(seed: int = 0) -> dict:
    key = jax.random.key(seed)
    ks = jax.random.split(key, 24)
    f32 = jnp.float32

    def nrm(k, shape, scale):
        return jax.random.normal(k, shape, f32) * scale

    x = nrm(ks[0], (BATCH, SEQ, D_MODEL), 1.0)
    c = nrm(ks[1], (BATCH, D_MODEL), 1.0)
    offset = jax.random.randint(ks[2], (BATCH, 1), 0, 1024, dtype=jnp.int32)
    positions = offset + jnp.arange(SEQ, dtype=jnp.int32)[None, :]
    rel_bias = nrm(ks[3], (NUM_BUCKETS, A_HEADS), 0.2)
    w_ada = nrm(ks[4], (DEPTH, D_MODEL, 6 * D_MODEL), 0.5 * D_MODEL ** -0.5)
    b_ada = nrm(ks[5], (DEPTH, 6 * D_MODEL), 0.01)
    w_in = nrm(ks[6], (DEPTH, D_MODEL, IN_WIDTH), D_MODEL ** -0.5)
    q_norm_g = 1.0 + nrm(ks[7], (DEPTH, Q_RANK), 0.01)
    w_uq = nrm(ks[8], (DEPTH, Q_RANK, B_HEADS * (NOPE_DIM + ROPE_DIM)), Q_RANK ** -0.5)
    kv_norm_g = 1.0 + nrm(ks[9], (DEPTH, KV_RANK), 0.01)
    w_ukv = nrm(ks[10], (DEPTH, KV_RANK, B_HEADS * (NOPE_DIM + V_DIM)), KV_RANK ** -0.5)
    w_branch_a = nrm(ks[11], (DEPTH, A_WIDTH, D_MODEL), A_WIDTH ** -0.5)
    w_branch_b = nrm(ks[12], (DEPTH, B_WIDTH, D_MODEL), B_WIDTH ** -0.5)
    w_out = nrm(ks[13], (DEPTH, D_MODEL, D_MODEL), DEEPNORM_BETA * D_MODEL ** -0.5)
    ln1_g = 1.0 + nrm(ks[14], (DEPTH, D_MODEL), 0.01)
    ln1_b = nrm(ks[15], (DEPTH, D_MODEL), 0.01)
    w_up = nrm(ks[16], (DEPTH, D_MODEL, 2 * D_FF), D_MODEL ** -0.5)
    conv_w = nrm(ks[17], (DEPTH, CONV_WIDTH, 2 * D_FF), CONV_WIDTH ** -0.5)
    conv_b = nrm(ks[18], (DEPTH, 2 * D_FF), 0.01)
    w_down = nrm(ks[19], (DEPTH, D_FF, D_MODEL), DEEPNORM_BETA * D_FF ** -0.5)
    ln2_g = 1.0 + nrm(ks[20], (DEPTH, D_MODEL), 0.01)
    ln2_b = nrm(ks[21], (DEPTH, D_MODEL), 0.01)
    return {"x": x, "c": c, "positions": positions, "rel_bias": rel_bias,
            "w_ada": w_ada, "b_ada": b_ada, "w_in": w_in,
            "q_norm_g": q_norm_g, "w_uq": w_uq, "kv_norm_g": kv_norm_g, "w_ukv": w_ukv,
            "w_branch_a": w_branch_a, "w_branch_b": w_branch_b, "w_out": w_out,
            "ln1_g": ln1_g, "ln1_b": ln1_b,
            "w_up": w_up, "conv_w": conv_w, "conv_b": conv_b, "w_down": w_down,
            "ln2_g": ln2_g, "ln2_b": ln2_b}


def reference(x, c, positions, rel_bias, w_ada, b_ada, w_in, q_norm_g, w_uq, kv_norm_g, w_ukv,
              w_branch_a, w_branch_b, w_out, ln1_g, ln1_b, w_up, conv_w, conv_b, w_down,
              ln2_g, ln2_b):
    for l in range(DEPTH):
        mod = jax.nn.silu(c) @ w_ada[l] + b_ada[l]
        sh1, sc1, g1, sh2, sc2, g2 = jnp.split(mod[:, None, :], 6, axis=-1)
        h = x * (1.0 + sc1) + sh1
        y = token_mixer(h, positions, rel_bias, w_in[l], q_norm_g[l], w_uq[l], kv_norm_g[l], w_ukv[l],
                        w_branch_a[l], w_branch_b[l], w_out[l])
        x = layer_norm(DEEPNORM_ALPHA * x + g1 * y, ln1_g[l], ln1_b[l])
        h = x * (1.0 + sc2) + sh2
        y = conv_ffn(h, w_up[l], conv_w[l], conv_b[l], w_down[l])
        x = layer_norm(DEEPNORM_ALPHA * x + g2 * y, ln2_g[l], ln2_b[l])
    return x
```

```python
import functools
import math

import numpy as np
import jax
import jax.numpy as jnp
from jax import lax
from jax.experimental import pallas as pl
from jax.experimental.pallas import tpu as pltpu

F32 = jnp.float32
BF16 = jnp.bfloat16

A_HEADS = 8
A_HEAD_DIM = 64
IDX_HEADS = 8
IDX_DIM = 64
TOPK_MAX = 256
B_HEADS = 8
Q_RANK = 256
KV_RANK = 128
NOPE_DIM = 64
ROPE_DIM = 32
ROPE_HALF = ROPE_DIM // 2
V_DIM = 64
ROPE_THETA = 10000.0
NUM_BUCKETS = 32
MAX_DISTANCE = 128
CONV_WIDTH = 3
LN_EPS = 1e-5
RMS_EPS = 1e-6
A_WIDTH = A_HEADS * A_HEAD_DIM
B_WIDTH = B_HEADS * V_DIM
B_QK_DIM = NOPE_DIM + ROPE_DIM

LANES = 128
V7X_VMEM_BYTES = 64 * 1024 * 1024
VMEM_LIMIT = 52 * 1024 * 1024

NEG = -1e30

TM_PROJ = 512
TM_FFN = 256
TQ = 256
TK = 512
IDX_CHUNK = 256


def _t5_bucket_table():
    n = np.arange(MAX_DISTANCE)
    max_exact = NUM_BUCKETS // 2
    out = []
    for dt in (np.float32, np.float64):
        ratio = np.log(np.maximum(n, 1).astype(dt) / dt(max_exact)) / dt(math.log(MAX_DISTANCE / max_exact))
        large = max_exact + (ratio * dt(NUM_BUCKETS - max_exact)).astype(np.int32)
        large = np.minimum(large, NUM_BUCKETS - 1)
        out.append(np.where(n < max_exact, n, large))
    assert (out[0] == out[1]).all()
    assert out[0][-1] == NUM_BUCKETS - 1
    return out[0].astype(np.int32)


_BUCKET_TABLE = _t5_bucket_table()


def _ada_kernel(c_ref, w_ref, b_ref, o_ref):
    c = c_ref[...]
    s = c * (1.0 / (1.0 + jnp.exp(-c)))
    o_ref[0] = jnp.dot(s, w_ref[0], preferred_element_type=F32) + b_ref[0]


def _ada_mod(c, w_ada, b_ada):
    depth, d, d6 = w_ada.shape
    bsz = c.shape[0]
    n_chunks = d6 // d
    return pl.pallas_call(
        _ada_kernel,
        grid=(depth, n_chunks),
        in_specs=[
            pl.BlockSpec((bsz, d), lambda l, j: (0, 0)),
            pl.BlockSpec((1, d, d), lambda l, j: (l, 0, j)),
            pl.BlockSpec((1, 1, d), lambda l, j: (l, 0, j)),
        ],
        out_specs=pl.BlockSpec((1, bsz, d), lambda l, j: (l, 0, j)),
        out_shape=jax.ShapeDtypeStruct((depth, bsz, d6), F32),
        name="ada_mod",
    )(c, w_ada, b_ada.reshape(depth, 1, d6))


_C_QKV = 0
_C_IQ = _C_QKV + 3 * A_WIDTH
_C_IK = _C_IQ + IDX_HEADS * IDX_DIM
_C_IW = _C_IK + LANES
_C_CQ = _C_IW + LANES
_C_CKV = _C_CQ + Q_RANK
_C_KR1 = _C_CKV + KV_RANK
_C_KR2 = _C_KR1 + LANES
_C_GATE = _C_KR2 + LANES


def _pack_w_in(w_in, d):
    sizes = (A_WIDTH, A_WIDTH, A_WIDTH, IDX_HEADS * IDX_DIM, IDX_DIM, IDX_HEADS,
             Q_RANK, KV_RANK, ROPE_DIM, d, d)
    offs = np.cumsum((0,) + sizes)
    seg = {name: w_in[:, offs[i]:offs[i + 1]] for i, name in enumerate(
        ("qa", "ka", "va", "iq", "ik", "iw", "cq", "ckv", "kr", "ga", "gb"))}

    def pad(w, width):
        return jnp.pad(w, ((0, 0), (0, width - w.shape[1])))

    parts = [seg["qa"], seg["ka"], seg["va"], seg["iq"],
             pad(seg["ik"], LANES), pad(seg["iw"], LANES),
             seg["cq"], seg["ckv"],
             pad(seg["kr"][:, :ROPE_HALF], LANES), pad(seg["kr"][:, ROPE_HALF:], LANES),
             seg["ga"], seg["gb"]]
    return jnp.concatenate(parts, axis=1).astype(BF16)


def _pack_w_uq(w_uq):
    w = w_uq.reshape(Q_RANK, B_HEADS, B_QK_DIM)
    nope = w[:, :, :NOPE_DIM].reshape(Q_RANK, B_HEADS * NOPE_DIM)
    r1 = w[:, :, NOPE_DIM:NOPE_DIM + ROPE_HALF].reshape(Q_RANK, B_HEADS * ROPE_HALF)
    r2 = w[:, :, NOPE_DIM + ROPE_HALF:].reshape(Q_RANK, B_HEADS * ROPE_HALF)
    return jnp.concatenate([nope, r1, r2], axis=1).astype(BF16)


def _pack_w_ukv(w_ukv):
    w = w_ukv.reshape(KV_RANK, B_HEADS, NOPE_DIM + V_DIM)
    kn = w[:, :, :NOPE_DIM].reshape(KV_RANK, B_HEADS * NOPE_DIM)
    v = w[:, :, NOPE_DIM:].reshape(KV_RANK, B_HEADS * V_DIM)
    return jnp.concatenate([kn, v], axis=1).astype(BF16)


def _sigmoid(x):
    return 1.0 / (1.0 + jnp.exp(-x))


def _rms(x, g):
    return x * lax.rsqrt(jnp.mean(x * x, axis=-1, keepdims=True) + RMS_EPS) * g


def _inproj_kernel(x_ref, sc_ref, sh_ref, pos_ref, invf_ref, w_ref, qg_ref, wuq_ref, kvg_ref, wukv_ref,
                   o_qkva, o_iq, o_ik, o_iw, o_qb, o_kvb, o_krb, o_gate, *, d_model):
    h = (x_ref[0] * (1.0 + sc_ref[0]) + sh_ref[0]).astype(BF16)

    def proj(c0, width):
        return jnp.dot(h, w_ref[:, c0:c0 + width], preferred_element_type=F32)

    a_scale = A_HEAD_DIM ** -0.5
    o_qkva[0, :, 0:A_WIDTH] = (proj(_C_QKV, A_WIDTH) * a_scale).astype(BF16)
    o_qkva[0, :, A_WIDTH:3 * A_WIDTH] = proj(_C_QKV + A_WIDTH, 2 * A_WIDTH).astype(BF16)
    o_iq[0] = proj(_C_IQ, IDX_HEADS * IDX_DIM).astype(BF16)
    o_ik[0] = proj(_C_IK, LANES)[:, :IDX_DIM].astype(BF16)
    o_iw[0] = proj(_C_IW, LANES)[:, :IDX_HEADS] * (IDX_DIM ** -0.5 * IDX_HEADS ** -0.5)

    ang = pos_ref[0].astype(F32) * invf_ref[...]
    cos, sin = jnp.cos(ang), jnp.sin(ang)

    b_scale = B_QK_DIM ** -0.5
    cq = _rms(proj(_C_CQ, Q_RANK), qg_ref[...]).astype(BF16)
    q = jnp.dot(cq, wuq_ref[...], preferred_element_type=F32)
    n_nope = B_HEADS * NOPE_DIM
    n_r = B_HEADS * ROPE_HALF
    x1 = q[:, n_nope:n_nope + n_r]
    x2 = q[:, n_nope + n_r:]
    o_qb[0, :, 0:n_nope] = (q[:, :n_nope] * b_scale).astype(BF16)
    o_qb[0, :, n_nope:n_nope + n_r] = ((x1 * cos - x2 * sin) * b_scale).astype(BF16)
    o_qb[0, :, n_nope + n_r:] = ((x1 * sin + x2 * cos) * b_scale).astype(BF16)

    ckv = _rms(proj(_C_CKV, KV_RANK), kvg_ref[...]).astype(BF16)
    o_kvb[0] = jnp.dot(ckv, wukv_ref[...], preferred_element_type=F32).astype(BF16)

    k1 = proj(_C_KR1, LANES)
    k2 = proj(_C_KR2, LANES)
    o_krb[0, :, 0:LANES] = (k1 * cos - k2 * sin).astype(BF16)
    o_krb[0, :, LANES:] = (k1 * sin + k2 * cos).astype(BF16)

    o_gate[0] = _sigmoid(proj(_C_GATE, 2 * d_model)).astype(BF16)


def _inproj(x, sc, sh, pos3, invf, w_pack, qg, wuq, kvg, wukv):
    bsz, seq, d = x.shape
    tm = min(TM_PROJ, seq)
    grid = (bsz, seq // tm)
    tok = lambda width: pl.BlockSpec((1, tm, width), lambda b, i: (b, i, 0))
    per_b = pl.BlockSpec((1, 1, d), lambda b, i: (b, 0, 0))
    full = lambda a: pl.BlockSpec(a.shape, lambda b, i: (0,) * a.ndim)
    out_widths = (3 * A_WIDTH, IDX_HEADS * IDX_DIM, IDX_DIM, IDX_HEADS,
                  B_HEADS * B_QK_DIM, B_HEADS * (NOPE_DIM + V_DIM), 2 * LANES, 2 * d)
    out_dtypes = (BF16, BF16, BF16, F32, BF16, BF16, BF16, BF16)
    return pl.pallas_call(
        functools.partial(_inproj_kernel, d_model=d),
        grid=grid,
        in_specs=[tok(d), per_b, per_b, tok(1), full(invf), full(w_pack), full(qg), full(wuq),
                  full(kvg), full(wukv)],
        out_specs=[tok(w) for w in out_widths],
        out_shape=[jax.ShapeDtypeStruct((bsz, seq, w), dt) for w, dt in zip(out_widths, out_dtypes)],
        compiler_params=pltpu.CompilerParams(
            dimension_semantics=("parallel", "parallel"), vmem_limit_bytes=VMEM_LIMIT),
        name="in_proj",
    )(x, sc, sh, pos3, invf, w_pack, qg, wuq, kvg, wukv)


def _ordered_key(x):
    b = lax.bitcast_convert_type(x, jnp.int32)
    return jnp.where(b < 0, b ^ jnp.int32(0x7FFFFFFF), b)


def _from_ordered_key(k):
    b = jnp.where(k < 0, k ^ jnp.int32(0x7FFFFFFF), k)
    return lax.bitcast_convert_type(b, F32)


def _indexer_kernel(iqT_ref, iwT_ref, ik_ref, o_ref, *, topk, seq):
    tq = o_ref.shape[2]
    ch = IDX_CHUNK
    qi = pl.program_id(1)
    n_chunks = (qi * tq) // ch + tq // ch
    sub = 8
    groups = ch // sub

    def score_chunk(c, carry):
        k0 = pl.multiple_of(c * ch, ch)
        ik = ik_ref[0, pl.ds(k0, ch), :]
        acc = jnp.zeros((ch, tq), F32)
        for hd in range(IDX_HEADS):
            y = jnp.dot(ik, iqT_ref[0, hd * IDX_DIM:(hd + 1) * IDX_DIM, :],
                        preferred_element_type=F32)
            acc = acc + jnp.maximum(y, 0.0) * iwT_ref[0, hd:hd + 1, :]
        key_idx = k0 + lax.broadcasted_iota(jnp.int32, (ch, tq), 0)
        q_idx = qi * tq + lax.broadcasted_iota(jnp.int32, (ch, tq), 1)
        o_ref[0, pl.ds(k0, ch), :] = jnp.where(key_idx <= q_idx, acc, -jnp.inf)
        return carry

    lax.fori_loop(0, n_chunks, score_chunk, 0)

    def reduce_rows(fn, init, combine, final):
        def body(c, part):
            k0 = pl.multiple_of(c * ch, ch)
            v = fn(o_ref[0, pl.ds(k0, ch), :], k0).reshape(groups, sub, tq)
            return combine(part, final(v, axis=0))
        part = lax.fori_loop(0, n_chunks, body, jnp.full((sub, tq), init, F32))
        return final(part, axis=0, keepdims=True)

    def count_ge(thr):
        return reduce_rows(lambda v, k0: jnp.where(v >= thr, 1.0, 0.0), 0.0, jnp.add, jnp.sum)

    n_valid = (qi * tq + lax.broadcasted_iota(jnp.int32, (1, tq), 1) + 1).astype(F32)
    kf = float(topk)

    row_max = reduce_rows(lambda v, k0: v, -jnp.inf, jnp.maximum, jnp.max)
    row_min = reduce_rows(lambda v, k0: jnp.where(v == -jnp.inf, jnp.inf, v), jnp.inf, jnp.minimum, jnp.min)

    lo_key0 = _ordered_key(row_min)
    hi_key0 = _ordered_key(row_max) + 1
    done0 = jnp.where(n_valid <= kf, 1, 0).astype(jnp.int32)
    thr0 = jnp.where(done0 > 0, -jnp.inf, row_min)

    def n_active(done):
        return jnp.sum(jnp.where(done > 0, 0.0, 1.0))

    def cond(st):
        return st[0] > 0.0

    def body(st):
        _, it, lo_key, hi_key, thr, done, tie = st
        lo = _from_ordered_key(lo_key)
        hi = _from_ordered_key(hi_key)
        mid_val_key = _ordered_key(lo * 0.5 + hi * 0.5)
        mid_bit_key = (lo_key & hi_key) + ((lo_key ^ hi_key) >> 1)
        use_val = (it < 4) & (jnp.abs(hi) < jnp.inf) & (jnp.abs(lo) < jnp.inf)
        mid_key = jnp.where(use_val, mid_val_key, mid_bit_key)
        mid_key = jnp.minimum(jnp.maximum(mid_key, lo_key + 1), hi_key - 1)
        mid = _from_ordered_key(mid_key)
        cnt = count_ge(mid)
        active = done == 0
        ge = cnt >= kf
        lo_key = jnp.where(active & ge, mid_key, lo_key)
        hi_key = jnp.where(active & jnp.logical_not(ge), mid_key, hi_key)
        hit = active & (cnt == kf)
        stuck = active & jnp.logical_not(hit) & (hi_key - 1 <= lo_key)
        thr = jnp.where(hit, mid, jnp.where(stuck, _from_ordered_key(lo_key), thr))
        tie = jnp.where(stuck, 1, tie)
        done = jnp.where(hit | stuck, 1, done)
        return n_active(done), it + 1, lo_key, hi_key, thr, done, tie

    st = lax.while_loop(cond, body, (n_active(done0), jnp.int32(0), lo_key0, hi_key0, thr0, done0,
                                     jnp.zeros((1, tq), jnp.int32)))
    thr, tie = st[4], st[6] > 0
    n_tie = jnp.sum(jnp.where(tie, 1.0, 0.0))

    def tie_cutoff():
        cnt_gt = reduce_rows(lambda v, k0: jnp.where(v > thr, 1.0, 0.0), 0.0, jnp.add, jnp.sum)
        need = kf - cnt_gt

        def count_tie_le(j):
            def fn(v, k0):
                idx = (k0 + lax.broadcasted_iota(jnp.int32, (ch, tq), 0)).astype(F32)
                return jnp.where((v == thr) & (idx <= j), 1.0, 0.0)
            return reduce_rows(fn, 0.0, jnp.add, jnp.sum)

        def jbody(_, lohi):
            jlo, jhi = lohi
            mid = jnp.floor((jlo + jhi) * 0.5)
            ok = count_tie_le(mid) >= need
            return jnp.where(ok, jlo, mid), jnp.where(ok, mid, jhi)

        n_iter = int(math.ceil(math.log2(seq))) + 1
        _, jhi = lax.fori_loop(0, n_iter, jbody,
                               (jnp.full((1, tq), -1.0, F32), jnp.full((1, tq), float(seq - 1), F32)))
        return jnp.where(tie, jhi, float(seq))

    cutoff = lax.cond(n_tie > 0.0, tie_cutoff, lambda: jnp.full((1, tq), float(seq), F32))

    def mask_chunk(c, carry):
        k0 = pl.multiple_of(c * ch, ch)
        v = o_ref[0, pl.ds(k0, ch), :]
        idx = (k0 + lax.broadcasted_iota(jnp.int32, (ch, tq), 0)).astype(F32)
        sel = (v > thr) | ((v == thr) & (idx <= cutoff))
        sel = sel & (v > -jnp.inf)
        o_ref[0, pl.ds(k0, ch), :] = jnp.where(sel, 0.0, NEG)
        return carry

    lax.fori_loop(0, n_chunks, mask_chunk, 0)

    def fill_chunk(c, carry):
        k0 = pl.multiple_of(c * ch, ch)
        o_ref[0, pl.ds(k0, ch), :] = jnp.full((ch, tq), NEG, F32)
        return carry

    lax.fori_loop(n_chunks, seq // ch, fill_chunk, 0)


def _indexer(iqT, iwT, ik, topk):
    bsz, _, seq = iqT.shape
    tq = min(TQ, seq)
    return pl.pallas_call(
        functools.partial(_indexer_kernel, topk=topk, seq=seq),
        grid=(bsz, seq // tq),
        in_specs=[
            pl.BlockSpec((1, IDX_HEADS * IDX_DIM, tq), lambda b, i: (b, 0, i)),
            pl.BlockSpec((1, IDX_HEADS, tq), lambda b, i: (b, 0, i)),
            pl.BlockSpec((1, seq, IDX_DIM), lambda b, i: (b, 0, 0)),
        ],
        out_specs=pl.BlockSpec((1, seq, tq), lambda b, i: (b, 0, i)),
        out_shape=jax.ShapeDtypeStruct((bsz, seq, seq), F32),
        compiler_params=pltpu.CompilerParams(
            dimension_semantics=("parallel", "parallel"), vmem_limit_bytes=VMEM_LIMIT),
        name="dsa_indexer",
    )(iqT, iwT, ik)


def _attn_kernel(*refs, n_heads, dk, dv, use_mask):
    if use_mask:
        (qmin_ref, kmax_ref, qT_ref, k_ref, vT_ref, mask_ref, qpos_ref, kpos_ref, btab_ref,
         o_ref, m_sc, l_sc, acc_sc) = refs
    else:
        qT_ref, k_ref, vT_ref, o_ref, m_sc, l_sc, acc_sc = refs
    tq = qT_ref.shape[2]
    tk = k_ref.shape[2]
    b = pl.program_id(0)
    qi = pl.program_id(1)
    ki = pl.program_id(2)
    nk = pl.num_programs(2)
    last_k = ((qi + 1) * tq - 1) // tk

    @pl.when(ki == 0)
    def _init():
        m_sc[...] = jnp.full(m_sc.shape, NEG, F32)
        l_sc[...] = jnp.zeros(l_sc.shape, F32)
        acc_sc[...] = jnp.zeros(acc_sc.shape, F32)

    def step(extra):
        for h in range(n_heads):
            s = jnp.dot(k_ref[0, h], qT_ref[0, h * dk:(h + 1) * dk, :], preferred_element_type=F32)
            s = extra(h, s)
            m_prev = m_sc[h:h + 1, :]
            m_new = jnp.maximum(m_prev, jnp.max(s, axis=0, keepdims=True))
            alpha = jnp.exp(m_prev - m_new)
            p = jnp.exp(s - m_new)
            l_sc[h:h + 1, :] = alpha * l_sc[h:h + 1, :] + jnp.sum(p, axis=0, keepdims=True)
            pv = jnp.dot(vT_ref[0, h * dv:(h + 1) * dv, :], p.astype(BF16), preferred_element_type=F32)
            acc_sc[h * dv:(h + 1) * dv, :] = acc_sc[h * dv:(h + 1) * dv, :] * alpha + pv
            m_sc[h:h + 1, :] = m_new

    if use_mask:
        near = qmin_ref[b, qi] - kmax_ref[b, ki] < MAX_DISTANCE
        active = ki <= last_k

        @pl.when(active & jnp.logical_not(near))
        def _far():
            step(lambda h, s: s + mask_ref[0])

        @pl.when(active & near)
        def _near():
            rel = qpos_ref[0] - kpos_ref[0]
            n = jnp.clip(rel, 0, MAX_DISTANCE - 1)

            def extra(h, s):
                tab = jnp.broadcast_to(btab_ref[h:h + 1, :], (tk, LANES))
                parts = [jnp.take_along_axis(tab, n[:, j * LANES:(j + 1) * LANES], axis=1)
                         for j in range(tq // LANES)]
                return s + mask_ref[0] + jnp.concatenate(parts, axis=1)
            step(extra)
    else:
        diag = (ki + 1) * tk > qi * tq

        @pl.when((ki <= last_k) & jnp.logical_not(diag))
        def _full():
            step(lambda h, s: s)

        @pl.when((ki <= last_k) & diag)
        def _diag():
            key_idx = ki * tk + lax.broadcasted_iota(jnp.int32, (tk, tq), 0)
            q_idx = qi * tq + lax.broadcasted_iota(jnp.int32, (tk, tq), 1)
            causal = key_idx <= q_idx
            step(lambda h, s: jnp.where(causal, s, NEG))

    @pl.when(ki == nk - 1)
    def _fin():
        for h in range(n_heads):
            inv = 1.0 / l_sc[h:h + 1, :]
            o_ref[0, h * dv:(h + 1) * dv, :] = (acc_sc[h * dv:(h + 1) * dv, :] * inv).astype(o_ref.dtype)


def _attention(qT, k, vT, n_heads, dk, dv, mask=None, pos=None, btab=None):
    bsz, _, seq = qT.shape
    tq = min(TQ, seq)
    tk = min(TK, seq)
    nq, nk = seq // tq, seq // tk
    use_mask = mask is not None

    def kclamp(qi, ki):
        return jnp.minimum(ki, ((qi + 1) * tq - 1) // tk)

    scratch = [pltpu.VMEM((n_heads, tq), F32), pltpu.VMEM((n_heads, tq), F32),
               pltpu.VMEM((n_heads * dv, tq), F32)]
    out_shape = jax.ShapeDtypeStruct((bsz, n_heads * dv, seq), BF16)
    kernel = functools.partial(_attn_kernel, n_heads=n_heads, dk=dk, dv=dv, use_mask=use_mask)
    params = pltpu.CompilerParams(dimension_semantics=("parallel", "parallel", "arbitrary"),
                                  vmem_limit_bytes=VMEM_LIMIT)
    if not use_mask:
        return pl.pallas_call(
            kernel,
            grid=(bsz, nq, nk),
            in_specs=[
                pl.BlockSpec((1, n_heads * dk, tq), lambda b, qi, ki: (b, 0, qi)),
                pl.BlockSpec((1, n_heads, tk, dk), lambda b, qi, ki: (b, 0, kclamp(qi, ki), 0)),
                pl.BlockSpec((1, n_heads * dv, tk), lambda b, qi, ki: (b, 0, kclamp(qi, ki))),
            ],
            out_specs=pl.BlockSpec((1, n_heads * dv, tq), lambda b, qi, ki: (b, 0, qi)),
            out_shape=out_shape,
            scratch_shapes=scratch,
            compiler_params=params,
            name="mla_attention",
        )(qT, k, vT)

    qpos = pos.reshape(bsz, 1, seq)
    kpos = pos.reshape(bsz, seq, 1)
    qmin = jnp.min(pos.reshape(bsz, nq, tq), axis=-1)
    kmax = jnp.max(pos.reshape(bsz, nk, tk), axis=-1)
    grid_spec = pltpu.PrefetchScalarGridSpec(
        num_scalar_prefetch=2,
        grid=(bsz, nq, nk),
        in_specs=[
            pl.BlockSpec((1, n_heads * dk, tq), lambda b, qi, ki, *_: (b, 0, qi)),
            pl.BlockSpec((1, n_heads, tk, dk), lambda b, qi, ki, *_: (b, 0, kclamp(qi, ki), 0)),
            pl.BlockSpec((1, n_heads * dv, tk), lambda b, qi, ki, *_: (b, 0, kclamp(qi, ki))),
            pl.BlockSpec((1, tk, tq), lambda b, qi, ki, *_: (b, kclamp(qi, ki), qi)),
            pl.BlockSpec((1, 1, tq), lambda b, qi, ki, *_: (b, 0, qi)),
            pl.BlockSpec((1, tk, 1), lambda b, qi, ki, *_: (b, kclamp(qi, ki), 0)),
            pl.BlockSpec(btab.shape, lambda b, qi, ki, *_: (0, 0)),
        ],
        out_specs=pl.BlockSpec((1, n_heads * dv, tq), lambda b, qi, ki, *_: (b, 0, qi)),
        scratch_shapes=scratch,
    )
    return pl.pallas_call(
        kernel, grid_spec=grid_spec, out_shape=out_shape, compiler_params=params,
        name="dsa_attention",
    )(qmin, kmax, qT, k, vT, mask, qpos, kpos, btab)


def _layer_norm(z, g, b):
    mu = jnp.mean(z, axis=-1, keepdims=True)
    zc = z - mu
    var = jnp.mean(zc * zc, axis=-1, keepdims=True)
    return zc * lax.rsqrt(var + LN_EPS) * g + b


def _merge_kernel(x_ref, ya_ref, yb_ref, gate_ref, g1_ref, wa_ref, wb_ref, wo_ref, lg_ref, lb_ref,
                  o_ref, *, alpha):
    d = x_ref.shape[2]
    pa = jnp.dot(ya_ref[0], wa_ref[...], preferred_element_type=F32)
    pb = jnp.dot(yb_ref[0], wb_ref[...], preferred_element_type=F32)
    gate = gate_ref[0].astype(F32)
    merged = gate[:, :d] * pa + gate[:, d:] * pb
    y = jnp.dot(merged.astype(BF16), wo_ref[...], preferred_element_type=F32)
    o_ref[0] = _layer_norm(alpha * x_ref[0] + g1_ref[0] * y, lg_ref[...], lb_ref[...])


def _merge(x, ya, yb, gate, g1, wa, wb, wo, lg, lb, alpha):
    bsz, seq, d = x.shape
    tm = min(TM_PROJ, seq)
    tok = lambda width: pl.BlockSpec((1, tm, width), lambda b, i: (b, i, 0))
    full = lambda a: pl.BlockSpec(a.shape, lambda b, i: (0,) * a.ndim)
    return pl.pallas_call(
        functools.partial(_merge_kernel, alpha=alpha),
        grid=(bsz, seq // tm),
        in_specs=[tok(d), tok(ya.shape[2]), tok(yb.shape[2]), tok(2 * d),
                  pl.BlockSpec((1, 1, d), lambda b, i: (b, 0, 0)),
                  full(wa), full(wb), full(wo), full(lg), full(lb)],
        out_specs=tok(d),
        out_shape=jax.ShapeDtypeStruct((bsz, seq, d), F32),
        compiler_params=pltpu.CompilerParams(
            dimension_semantics=("parallel", "parallel"), vmem_limit_bytes=VMEM_LIMIT),
        name="merge_out_ln",
    )(x, ya, yb, gate, g1, wa, wb, wo, lg, lb)


def _ffn_kernel(x_ref, sc_ref, sh_ref, g2_ref, wup_ref, cw_ref, cb_ref, wdn_ref, lg_ref, lb_ref,
                o_ref, hist_ref, *, alpha, d_ff):
    i = pl.program_id(1)
    tm = x_ref.shape[1]
    x = x_ref[0]
    h = (x * (1.0 + sc_ref[0]) + sh_ref[0]).astype(BF16)
    u = jnp.dot(h, wup_ref[...], preferred_element_type=F32)

    @pl.when(i == 0)
    def _():
        hist_ref[...] = jnp.zeros(hist_ref.shape, F32)

    row = lax.broadcasted_iota(jnp.int32, u.shape, 0)
    prev = hist_ref[...]
    u1 = pltpu.roll(u, 1, axis=0)
    u1 = jnp.where(row == 0, prev[7:8, :], u1)
    u2 = pltpu.roll(u, 2, axis=0)
    u2 = jnp.where(row == 0, prev[6:7, :], jnp.where(row == 1, prev[7:8, :], u2))
    hist_ref[...] = u[tm - 8:, :]
    cw = cw_ref[...]
    uc = cw[0:1, :] * u2 + cw[1:2, :] * u1 + cw[2:3, :] * u + cb_ref[...]
    g = uc[:, :d_ff]
    val = uc[:, d_ff:]
    a = (g * _sigmoid(g) * val).astype(BF16)
    y = jnp.dot(a, wdn_ref[...], preferred_element_type=F32)
    o_ref[0] = _layer_norm(alpha * x + g2_ref[0] * y, lg_ref[...], lb_ref[...])


def _ffn(x, sc, sh, g2, wup, cw, cb, wdn, lg, lb, alpha):
    bsz, seq, d = x.shape
    d_ff = wdn.shape[0]
    tm = min(TM_FFN, seq)
    tok = pl.BlockSpec((1, tm, d), lambda b, i: (b, i, 0))
    per_b = pl.BlockSpec((1, 1, d), lambda b, i: (b, 0, 0))
    once = lambda a: pl.BlockSpec(a.shape, lambda b, i: (0,) * a.ndim, pipeline_mode=pl.Buffered(1))
    return pl.pallas_call(
        functools.partial(_ffn_kernel, alpha=alpha, d_ff=d_ff),
        grid=(bsz, seq // tm),
        in_specs=[tok, per_b, per_b, per_b, once(wup), once(cw), once(cb), once(wdn), once(lg), once(lb)],
        out_specs=tok,
        out_shape=jax.ShapeDtypeStruct((bsz, seq, d), F32),
        scratch_shapes=[pltpu.VMEM((8, 2 * d_ff), F32)],
        compiler_params=pltpu.CompilerParams(
            dimension_semantics=("parallel", "arbitrary"), vmem_limit_bytes=VMEM_LIMIT),
        name="conv_ffn_ln",
    )(x, sc, sh, g2, wup, cw, cb, wdn, lg, lb)


def _heads_major(a, n_heads):
    bsz, seq, width = a.shape
    return a.reshape(bsz, seq, n_heads, width // n_heads).transpose(0, 2, 1, 3)


def kernel(x, c, positions, rel_bias, w_ada, b_ada, w_in, q_norm_g, w_uq, kv_norm_g, w_ukv,
           w_branch_a, w_branch_b, w_out, ln1_g, ln1_b, w_up, conv_w, conv_b, w_down, ln2_g, ln2_b):
    bsz, seq, d = x.shape
    depth = w_ada.shape[0]
    alpha = (2 * depth) ** 0.25
    topk = min(TOPK_MAX, seq // 4)
    assert seq % TQ == 0 or seq < TQ

    mod = _ada_mod(c, w_ada, b_ada)
    pos3 = positions.reshape(bsz, seq, 1)
    invf = ROPE_THETA ** (-jnp.arange(ROPE_HALF, dtype=F32) * (2.0 / ROPE_DIM))
    invf = jnp.tile(invf, LANES // ROPE_HALF).reshape(1, LANES)
    btab = (rel_bias[_BUCKET_TABLE] - rel_bias[NUM_BUCKETS - 1][None, :]).T.astype(F32)

    for l in range(depth):
        sh1, sc1, g1, sh2, sc2, g2 = [mod[l, :, None, j * d:(j + 1) * d] for j in range(6)]
        qkva, iq, ik, iw, qb, kvb, krb, gate = _inproj(
            x, sc1, sh1, pos3, invf, _pack_w_in(w_in[l], d),
            q_norm_g[l].reshape(1, -1), _pack_w_uq(w_uq[l]),
            kv_norm_g[l].reshape(1, -1), _pack_w_ukv(w_ukv[l]))

        mask = _indexer(iq.transpose(0, 2, 1), iw.transpose(0, 2, 1), ik, topk)
        qaT = qkva[:, :, :A_WIDTH].transpose(0, 2, 1)
        ka = _heads_major(qkva[:, :, A_WIDTH:2 * A_WIDTH], A_HEADS)
        vaT = qkva[:, :, 2 * A_WIDTH:].transpose(0, 2, 1)
        yaT = _attention(qaT, ka, vaT, A_HEADS, A_HEAD_DIM, A_HEAD_DIM, mask=mask, pos=positions, btab=btab)

        n_nope = B_HEADS * NOPE_DIM
        n_r = B_HEADS * ROPE_HALF
        q_heads = jnp.concatenate([
            qb[:, :, :n_nope].reshape(bsz, seq, B_HEADS, NOPE_DIM),
            qb[:, :, n_nope:n_nope + n_r].reshape(bsz, seq, B_HEADS, ROPE_HALF),
            qb[:, :, n_nope + n_r:].reshape(bsz, seq, B_HEADS, ROPE_HALF)], axis=-1)
        qbT = q_heads.reshape(bsz, seq, B_HEADS * B_QK_DIM).transpose(0, 2, 1)
        kr = jnp.concatenate([krb[:, :, :ROPE_HALF], krb[:, :, LANES:LANES + ROPE_HALF]], axis=-1)
        k_heads = jnp.concatenate([
            kvb[:, :, :n_nope].reshape(bsz, seq, B_HEADS, NOPE_DIM),
            jnp.broadcast_to(kr[:, :, None, :], (bsz, seq, B_HEADS, ROPE_DIM))], axis=-1)
        kb = k_heads.transpose(0, 2, 1, 3)
        vbT = kvb[:, :, n_nope:].transpose(0, 2, 1)
        ybT = _attention(qbT, kb, vbT, B_HEADS, B_QK_DIM, V_DIM)

        x = _merge(x, yaT.transpose(0, 2, 1), ybT.transpose(0, 2, 1), gate, g1,
                   w_branch_a[l].astype(BF16), w_branch_b[l].astype(BF16), w_out[l].astype(BF16),
                   ln1_g[l].reshape(1, d), ln1_b[l].reshape(1, d), alpha)
        x = _ffn(x, sc2, sh2, g2, w_up[l].astype(BF16), conv_w[l], conv_b[l].reshape(1, -1),
                 w_down[l].astype(BF16), ln2_g[l].reshape(1, d), ln2_b[l].reshape(1, d), alpha)
    return x
```

```python
import functools
import math

import numpy as np
import jax
import jax.numpy as jnp
from jax import lax
from jax.experimental import pallas as pl
from jax.experimental.pallas import tpu as pltpu

F32 = jnp.float32
BF16 = jnp.bfloat16

A_HEADS = 8
A_HEAD_DIM = 64
IDX_HEADS = 8
IDX_DIM = 64
TOPK_MAX = 256
B_HEADS = 8
Q_RANK = 256
KV_RANK = 128
NOPE_DIM = 64
ROPE_DIM = 32
ROPE_HALF = ROPE_DIM // 2
V_DIM = 64
ROPE_THETA = 10000.0
NUM_BUCKETS = 32
MAX_DISTANCE = 128
CONV_WIDTH = 3
LN_EPS = 1e-5
RMS_EPS = 1e-6
A_WIDTH = A_HEADS * A_HEAD_DIM
B_WIDTH = B_HEADS * V_DIM
B_QK_DIM = NOPE_DIM + ROPE_DIM

LANES = 128
V7X_VMEM_BYTES = 64 * 1024 * 1024
VMEM_LIMIT = 52 * 1024 * 1024

NEG = -1e30
LOG2E = math.log2(math.e)

TM_PROJ = 512
TM_FFN = 256
TQ = 256
TK = 512
IDX_CHUNK = 256
ATTN_CHUNK = 128
ONES_ROWS = 16


def _t5_bucket_table():
    n = np.arange(MAX_DISTANCE)
    max_exact = NUM_BUCKETS // 2
    out = []
    for dt in (np.float32, np.float64):
        ratio = np.log(np.maximum(n, 1).astype(dt) / dt(max_exact)) / dt(math.log(MAX_DISTANCE / max_exact))
        large = max_exact + (ratio * dt(NUM_BUCKETS - max_exact)).astype(np.int32)
        large = np.minimum(large, NUM_BUCKETS - 1)
        out.append(np.where(n < max_exact, n, large))
    assert (out[0] == out[1]).all()
    assert out[0][-1] == NUM_BUCKETS - 1
    return out[0].astype(np.int32)


_BUCKET_TABLE = _t5_bucket_table()


def _ada_kernel(c_ref, w_ref, b_ref, o_ref):
    c = c_ref[...]
    s = c * (1.0 / (1.0 + jnp.exp(-c)))
    o_ref[0] = jnp.dot(s, w_ref[0], preferred_element_type=F32) + b_ref[0]


def _ada_mod(c, w_ada, b_ada):
    depth, d, d6 = w_ada.shape
    bsz = c.shape[0]
    n_chunks = d6 // d
    return pl.pallas_call(
        _ada_kernel,
        grid=(depth, n_chunks),
        in_specs=[
            pl.BlockSpec((bsz, d), lambda l, j: (0, 0)),
            pl.BlockSpec((1, d, d), lambda l, j: (l, 0, j)),
            pl.BlockSpec((1, 1, d), lambda l, j: (l, 0, j)),
        ],
        out_specs=pl.BlockSpec((1, bsz, d), lambda l, j: (l, 0, j)),
        out_shape=jax.ShapeDtypeStruct((depth, bsz, d6), F32),
        name="ada_mod",
    )(c, w_ada, b_ada.reshape(depth, 1, d6))


_C_QKV = 0
_C_IQ = _C_QKV + 3 * A_WIDTH
_C_IK = _C_IQ + IDX_HEADS * IDX_DIM
_C_IW = _C_IK + LANES
_C_CQ = _C_IW + LANES
_C_CKV = _C_CQ + Q_RANK
_C_KR1 = _C_CKV + KV_RANK
_C_KR2 = _C_KR1 + LANES
_C_GATE = _C_KR2 + LANES


def _pack_w_in(w_in, d):
    sizes = (A_WIDTH, A_WIDTH, A_WIDTH, IDX_HEADS * IDX_DIM, IDX_DIM, IDX_HEADS,
             Q_RANK, KV_RANK, ROPE_DIM, d, d)
    offs = np.cumsum((0,) + sizes)
    seg = {name: w_in[:, offs[i]:offs[i + 1]] for i, name in enumerate(
        ("qa", "ka", "va", "iq", "ik", "iw", "cq", "ckv", "kr", "ga", "gb"))}

    def pad(w, width):
        return jnp.pad(w, ((0, 0), (0, width - w.shape[1])))

    parts = [seg["qa"], seg["ka"], seg["va"], seg["iq"],
             pad(seg["ik"], LANES), pad(seg["iw"], LANES),
             seg["cq"], seg["ckv"],
             pad(seg["kr"][:, :ROPE_HALF], LANES), pad(seg["kr"][:, ROPE_HALF:], LANES),
             seg["ga"], seg["gb"]]
    return jnp.concatenate(parts, axis=1).astype(BF16)


def _pack_w_uq(w_uq):
    w = w_uq.reshape(Q_RANK, B_HEADS, B_QK_DIM)
    nope = w[:, :, :NOPE_DIM].reshape(Q_RANK, B_HEADS * NOPE_DIM)
    r1 = w[:, :, NOPE_DIM:NOPE_DIM + ROPE_HALF].reshape(Q_RANK, B_HEADS * ROPE_HALF)
    r2 = w[:, :, NOPE_DIM + ROPE_HALF:].reshape(Q_RANK, B_HEADS * ROPE_HALF)
    return jnp.concatenate([nope, r1, r2], axis=1).astype(BF16)


def _pack_w_ukv(w_ukv):
    w = w_ukv.reshape(KV_RANK, B_HEADS, NOPE_DIM + V_DIM)
    kn = w[:, :, :NOPE_DIM].reshape(KV_RANK, B_HEADS * NOPE_DIM)
    v = w[:, :, NOPE_DIM:].reshape(KV_RANK, B_HEADS * V_DIM)
    return jnp.concatenate([kn, v], axis=1).astype(BF16)


def _sigmoid(x):
    return 1.0 / (1.0 + jnp.exp(-x))


def _rms(x, g):
    return x * lax.rsqrt(jnp.mean(x * x, axis=-1, keepdims=True) + RMS_EPS) * g


def _inproj_kernel(x_ref, sc_ref, sh_ref, pos_ref, invf_ref, w_ref, qg_ref, wuq_ref, kvg_ref, wukv_ref,
                   o_qkva, o_iq, o_ik, o_iw, o_qb, o_kvb, o_krb, o_gate, *, d_model):
    h = (x_ref[0] * (1.0 + sc_ref[0]) + sh_ref[0]).astype(BF16)

    def proj(c0, width):
        return jnp.dot(h, w_ref[:, c0:c0 + width], preferred_element_type=F32)

    a_scale = A_HEAD_DIM ** -0.5 * LOG2E
    o_qkva[0, :, 0:A_WIDTH] = (proj(_C_QKV, A_WIDTH) * a_scale).astype(BF16)
    o_qkva[0, :, A_WIDTH:3 * A_WIDTH] = proj(_C_QKV + A_WIDTH, 2 * A_WIDTH).astype(BF16)
    o_iq[0] = proj(_C_IQ, IDX_HEADS * IDX_DIM).astype(BF16)
    o_ik[0] = proj(_C_IK, LANES)[:, :IDX_DIM].astype(BF16)
    o_iw[0] = proj(_C_IW, LANES)[:, :IDX_HEADS] * (IDX_DIM ** -0.5 * IDX_HEADS ** -0.5)

    ang = pos_ref[0].astype(F32) * invf_ref[...]
    cos, sin = jnp.cos(ang), jnp.sin(ang)

    b_scale = B_QK_DIM ** -0.5 * LOG2E
    cq = _rms(proj(_C_CQ, Q_RANK), qg_ref[...]).astype(BF16)
    q = jnp.dot(cq, wuq_ref[...], preferred_element_type=F32)
    n_nope = B_HEADS * NOPE_DIM
    n_r = B_HEADS * ROPE_HALF
    x1 = q[:, n_nope:n_nope + n_r]
    x2 = q[:, n_nope + n_r:]
    o_qb[0, :, 0:n_nope] = (q[:, :n_nope] * b_scale).astype(BF16)
    o_qb[0, :, n_nope:n_nope + n_r] = ((x1 * cos - x2 * sin) * b_scale).astype(BF16)
    o_qb[0, :, n_nope + n_r:] = ((x1 * sin + x2 * cos) * b_scale).astype(BF16)

    ckv = _rms(proj(_C_CKV, KV_RANK), kvg_ref[...]).astype(BF16)
    o_kvb[0] = jnp.dot(ckv, wukv_ref[...], preferred_element_type=F32).astype(BF16)

    k1 = proj(_C_KR1, LANES)
    k2 = proj(_C_KR2, LANES)
    o_krb[0, :, 0:LANES] = (k1 * cos - k2 * sin).astype(BF16)
    o_krb[0, :, LANES:] = (k1 * sin + k2 * cos).astype(BF16)

    o_gate[0] = _sigmoid(proj(_C_GATE, 2 * d_model)).astype(BF16)


def _inproj(x, sc, sh, pos3, invf, w_pack, qg, wuq, kvg, wukv):
    bsz, seq, d = x.shape
    tm = min(TM_PROJ, seq)
    grid = (bsz, seq // tm)
    tok = lambda width: pl.BlockSpec((1, tm, width), lambda b, i: (b, i, 0))
    per_b = pl.BlockSpec((1, 1, d), lambda b, i: (b, 0, 0))
    full = lambda a: pl.BlockSpec(a.shape, lambda b, i: (0,) * a.ndim)
    out_widths = (3 * A_WIDTH, IDX_HEADS * IDX_DIM, IDX_DIM, IDX_HEADS,
                  B_HEADS * B_QK_DIM, B_HEADS * (NOPE_DIM + V_DIM), 2 * LANES, 2 * d)
    out_dtypes = (BF16, BF16, BF16, F32, BF16, BF16, BF16, BF16)
    return pl.pallas_call(
        functools.partial(_inproj_kernel, d_model=d),
        grid=grid,
        in_specs=[tok(d), per_b, per_b, tok(1), full(invf), full(w_pack), full(qg), full(wuq),
                  full(kvg), full(wukv)],
        out_specs=[tok(w) for w in out_widths],
        out_shape=[jax.ShapeDtypeStruct((bsz, seq, w), dt) for w, dt in zip(out_widths, out_dtypes)],
        compiler_params=pltpu.CompilerParams(
            dimension_semantics=("parallel", "parallel"), vmem_limit_bytes=VMEM_LIMIT),
        name="in_proj",
    )(x, sc, sh, pos3, invf, w_pack, qg, wuq, kvg, wukv)


def _ordered_key(x):
    b = lax.bitcast_convert_type(x, jnp.int32)
    return jnp.where(b < 0, b ^ jnp.int32(0x7FFFFFFF), b)


def _from_ordered_key(k):
    b = jnp.where(k < 0, k ^ jnp.int32(0x7FFFFFFF), k)
    return lax.bitcast_convert_type(b, F32)


def _indexer_kernel(iqT_ref, iwT_ref, ik_ref, o_ref, tcnt_ref, *, topk, seq):
    tq = o_ref.shape[2]
    ch = IDX_CHUNK
    qi = pl.program_id(1)
    n_chunks = (qi * tq) // ch + tq // ch
    sub = 8
    kf = float(topk)

    def fold(v, op):
        return op(v.reshape(v.shape[0] // sub, sub, tq), axis=0)

    def count(pred):
        return fold(jnp.where(pred, 1.0, 0.0), jnp.sum)

    def chunk(c):
        k0 = pl.multiple_of(c * ch, ch)
        return k0, o_ref[0, pl.ds(k0, ch), :]

    def score_chunk(c, carry, diagonal):
        vmax, vmin, n_pos, n_nonneg = carry
        k0 = pl.multiple_of(c * ch, ch)
        ik = ik_ref[0, pl.ds(k0, ch), :]
        acc = jnp.zeros((ch, tq), F32)
        for hd in range(IDX_HEADS):
            y = jnp.dot(ik, iqT_ref[0, hd * IDX_DIM:(hd + 1) * IDX_DIM, :],
                        preferred_element_type=F32)
            acc = acc + jnp.maximum(y, 0.0) * iwT_ref[0, hd:hd + 1, :]
        if diagonal:
            key_idx = k0 + lax.broadcasted_iota(jnp.int32, (ch, tq), 0)
            q_idx = qi * tq + lax.broadcasted_iota(jnp.int32, (ch, tq), 1)
            causal = key_idx <= q_idx
            acc = jnp.where(causal, acc, -jnp.inf)
            vmin = jnp.minimum(vmin, fold(jnp.where(causal, acc, jnp.inf), jnp.min))
        else:
            vmin = jnp.minimum(vmin, fold(acc, jnp.min))
        o_ref[0, pl.ds(k0, ch), :] = acc
        vmax = jnp.maximum(vmax, fold(acc, jnp.max))
        n_pos = n_pos + count(acc > 0.0)
        n_nonneg = n_nonneg + count(acc >= 0.0)
        return vmax, vmin, n_pos, n_nonneg

    n_diag = tq // ch
    stats = lax.fori_loop(0, n_chunks - n_diag, functools.partial(score_chunk, diagonal=False),
                          (jnp.full((sub, tq), -jnp.inf, F32), jnp.full((sub, tq), jnp.inf, F32),
                           jnp.zeros((sub, tq), F32), jnp.zeros((sub, tq), F32)))
    stats = lax.fori_loop(n_chunks - n_diag, n_chunks, functools.partial(score_chunk, diagonal=True), stats)
    row_max = jnp.max(stats[0], axis=0, keepdims=True)
    row_min = jnp.min(stats[1], axis=0, keepdims=True)
    n_pos = jnp.sum(stats[2], axis=0, keepdims=True)
    n_nonneg = jnp.sum(stats[3], axis=0, keepdims=True)

    def count_ge(thr):
        def body(c, part):
            _, v = chunk(c)
            return part + count(v >= thr)
        part = lax.fori_loop(0, n_chunks, body, jnp.zeros((sub, tq), F32))
        return jnp.sum(part, axis=0, keepdims=True)

    n_valid = (qi * tq + lax.broadcasted_iota(jnp.int32, (1, tq), 1) + 1).astype(F32)

    keep_all = n_valid <= kf
    zero_tie = jnp.logical_not(keep_all) & (n_pos < kf) & (n_nonneg >= kf)
    positive = n_pos >= kf
    lo_key0 = jnp.where(positive, 1, _ordered_key(row_min))
    hi_key0 = jnp.where(positive, _ordered_key(row_max) + 1, -2)
    done0 = jnp.where(keep_all | zero_tie, 1, 0).astype(jnp.int32)
    thr0 = jnp.where(keep_all, -jnp.inf, jnp.where(zero_tie, 0.0, row_min))
    tie0 = jnp.where(zero_tie, 1, 0).astype(jnp.int32)

    def n_active(done):
        return jnp.sum(jnp.where(done > 0, 0.0, 1.0))

    def cond(st):
        return st[0] > 0.0

    def body(st):
        _, it, lo_key, hi_key, thr, done, tie = st
        lo = _from_ordered_key(lo_key)
        hi = _from_ordered_key(hi_key)
        mid_val_key = _ordered_key(lo * 0.5 + hi * 0.5)
        mid_bit_key = (lo_key & hi_key) + ((lo_key ^ hi_key) >> 1)
        use_val = (it < 4) & (jnp.abs(hi) < jnp.inf) & (jnp.abs(lo) < jnp.inf)
        mid_key = jnp.where(use_val, mid_val_key, mid_bit_key)
        mid_key = jnp.minimum(jnp.maximum(mid_key, lo_key + 1), hi_key - 1)
        mid = _from_ordered_key(mid_key)
        cnt = count_ge(mid)
        active = done == 0
        ge = cnt >= kf
        lo_key = jnp.where(active & ge, mid_key, lo_key)
        hi_key = jnp.where(active & jnp.logical_not(ge), mid_key, hi_key)
        hit = active & (cnt == kf)
        stuck = active & jnp.logical_not(hit) & (hi_key - 1 <= lo_key)
        thr = jnp.where(hit, mid, jnp.where(stuck, _from_ordered_key(lo_key), thr))
        tie = jnp.where(stuck, 1, tie)
        done = jnp.where(hit | stuck, 1, done)
        return n_active(done), it + 1, lo_key, hi_key, thr, done, tie

    st = lax.while_loop(cond, body, (n_active(done0), jnp.int32(0), lo_key0, hi_key0, thr0, done0, tie0))
    thr, tie = st[4], st[6] > 0
    n_tie = jnp.sum(jnp.where(tie, 1.0, 0.0))

    def write_mask(k0, sel):
        o_ref[0, pl.ds(k0, ch), :] = jnp.where(sel, 0.0, NEG)

    def mask_plain():
        def body(c, carry):
            k0, v = chunk(c)
            write_mask(k0, (v >= thr) & (v > -jnp.inf))
            return carry
        lax.fori_loop(0, n_chunks, body, 0)

    def mask_ties():
        def count_body(c, n_gt):
            _, v = chunk(c)
            tcnt_ref[c] = count(v == thr)
            return n_gt + count(v > thr)
        n_gt = lax.fori_loop(0, n_chunks, count_body, jnp.zeros((sub, tq), F32))
        n_gt = jnp.sum(n_gt, axis=0, keepdims=True)
        need = jnp.where(tie, kf - n_gt, float(seq))

        def quota_body(c, carry):
            before, split = carry
            t = jnp.sum(tcnt_ref[c], axis=0, keepdims=True)
            keep = jnp.clip(need - before, 0.0, t)
            tcnt_ref[c, 0:1, :] = keep
            split = jnp.where((keep > 0.0) & (keep < t), c.astype(F32), split)
            return before + t, split
        _, split0 = lax.fori_loop(0, n_chunks, quota_body,
                                  (jnp.zeros((1, tq), F32), jnp.full((1, tq), -1.0, F32)))

        def split_body(st):
            c_f, split = st
            c = c_f.astype(jnp.int32)
            k0, v = chunk(c)
            is_tie = v == thr
            r = lax.broadcasted_iota(jnp.int32, (ch, ch), 0)
            s = lax.broadcasted_iota(jnp.int32, (ch, ch), 1)
            lower = jnp.where(s < r, 1.0, 0.0).astype(BF16)
            rank = jnp.dot(lower, jnp.where(is_tie, 1.0, 0.0).astype(BF16), preferred_element_type=F32)
            o_ref[0, pl.ds(k0, ch), :] = jnp.where(is_tie & (rank >= tcnt_ref[c, 0:1, :]), -jnp.inf, v)
            split = jnp.where(split == c_f, -1.0, split)
            return jnp.max(split), split
        lax.while_loop(lambda st: st[0] >= 0.0, split_body, (jnp.max(split0), split0))

        def body(c, carry):
            k0, v = chunk(c)
            keep_ties = tcnt_ref[c, 0:1, :] > 0.0
            write_mask(k0, ((v > thr) | ((v == thr) & keep_ties)) & (v > -jnp.inf))
            return carry
        lax.fori_loop(0, n_chunks, body, 0)

    lax.cond(n_tie > 0.0, mask_ties, mask_plain)

    def fill_chunk(c, carry):
        k0 = pl.multiple_of(c * ch, ch)
        o_ref[0, pl.ds(k0, ch), :] = jnp.full((ch, tq), NEG, F32)
        return carry

    lax.fori_loop(n_chunks, seq // ch, fill_chunk, 0)


def _indexer(iqT, iwT, ik, topk):
    bsz, _, seq = iqT.shape
    tq = min(TQ, seq)
    return pl.pallas_call(
        functools.partial(_indexer_kernel, topk=topk, seq=seq),
        grid=(bsz, seq // tq),
        in_specs=[
            pl.BlockSpec((1, IDX_HEADS * IDX_DIM, tq), lambda b, i: (b, 0, i)),
            pl.BlockSpec((1, IDX_HEADS, tq), lambda b, i: (b, 0, i)),
            pl.BlockSpec((1, seq, IDX_DIM), lambda b, i: (b, 0, 0)),
        ],
        out_specs=pl.BlockSpec((1, seq, tq), lambda b, i: (b, 0, i)),
        out_shape=jax.ShapeDtypeStruct((bsz, seq, seq), F32),
        scratch_shapes=[pltpu.VMEM((max(seq // IDX_CHUNK, 1), 8, tq), F32)],
        compiler_params=pltpu.CompilerParams(
            dimension_semantics=("parallel", "parallel"), vmem_limit_bytes=VMEM_LIMIT),
        name="dsa_indexer",
    )(iqT, iwT, ik)


def _attn_kernel(*refs, n_heads, dk, dv, use_mask):
    if use_mask:
        (pq_ref, pk_ref, qmin_ref, kmax_ref, qT_ref, k_ref, vT_ref, mask_ref, qpos_ref, kpos_ref, btab_ref,
         o_ref, m_sc, acc_sc, s_buf, p_buf) = refs
    else:
        pq_ref, pk_ref, qT_ref, k_ref, vT_ref, o_ref, m_sc, acc_sc, s_buf, p_buf = refs
    tq = qT_ref.shape[2]
    tk = k_ref.shape[2]
    b = pl.program_id(0)
    pair = pl.program_id(1)
    qi = pq_ref[pair]
    ki = pk_ref[pair]
    last_k = ((qi + 1) * tq - 1) // tk

    @pl.when(ki == 0)
    def _init():
        m_sc[...] = jnp.full(m_sc.shape, NEG, F32)
        acc_sc[...] = jnp.zeros(acc_sc.shape, F32)

    ck = min(ATTN_CHUNK, tk)
    n_chunks = tk // ck
    dva = vT_ref.shape[1] // n_heads
    rows = lambda c: slice(c * ck, (c + 1) * ck)

    def step(extra):
        def logits(h, c):
            s = jnp.dot(k_ref[0, h, rows(c), :], qT_ref[0, h * dk:(h + 1) * dk, :],
                        preferred_element_type=F32)
            return extra(h, c, s)

        def logits_to_buf(h):
            part = None
            for c in range(n_chunks):
                s = logits(h, c)
                s_buf[h % 2, rows(c), :] = s
                part = s if part is None else jnp.maximum(part, s)
            return jnp.max(part, axis=0, keepdims=True)

        tile_max = logits_to_buf(0)
        for h in range(n_heads):
            m_prev = m_sc[h:h + 1, :]
            m_new = jnp.maximum(m_prev, tile_max)
            alpha = jnp.exp2(m_prev - m_new)
            if h + 1 < n_heads:
                tile_max = logits_to_buf(h + 1)
            for c in range(n_chunks):
                p_buf[h % 2, rows(c), :] = jnp.exp2(s_buf[h % 2, rows(c), :] - m_new).astype(BF16)
            pv = jnp.dot(vT_ref[0, h * dva:(h + 1) * dva, :], p_buf[h % 2], preferred_element_type=F32)
            acc_sc[h * dva:(h + 1) * dva, :] = acc_sc[h * dva:(h + 1) * dva, :] * alpha + pv
            m_sc[h:h + 1, :] = m_new

    if use_mask:
        near = qmin_ref[b, qi] - kmax_ref[b, ki] < MAX_DISTANCE

        @pl.when(jnp.logical_not(near))
        def _far():
            step(lambda h, c, s: s + mask_ref[0, rows(c), :])

        @pl.when(near)
        def _near():
            def extra(h, c, s):
                rel = qpos_ref[0] - kpos_ref[0, rows(c), :]
                n = jnp.clip(rel, 0, MAX_DISTANCE - 1)
                tab = jnp.broadcast_to(btab_ref[h:h + 1, :], (ck, LANES))
                parts = [jnp.take_along_axis(tab, n[:, j * LANES:(j + 1) * LANES], axis=1)
                         for j in range(tq // LANES)]
                return s + mask_ref[0, rows(c), :] + jnp.concatenate(parts, axis=1)
            step(extra)
    else:
        diag = (ki + 1) * tk > qi * tq

        @pl.when(jnp.logical_not(diag))
        def _full():
            step(lambda h, c, s: s)

        @pl.when(diag)
        def _diag():
            def extra(h, c, s):
                key_idx = ki * tk + c * ck + lax.broadcasted_iota(jnp.int32, (ck, tq), 0)
                q_idx = qi * tq + lax.broadcasted_iota(jnp.int32, (ck, tq), 1)
                return jnp.where(key_idx <= q_idx, s, NEG)
            step(extra)

    @pl.when(ki == last_k)
    def _fin():
        for h in range(n_heads):
            inv = 1.0 / acc_sc[h * dva + dv:h * dva + dv + 1, :]
            o_ref[0, h * dv:(h + 1) * dv, :] = (acc_sc[h * dva:h * dva + dv, :] * inv).astype(o_ref.dtype)


def _causal_pairs(nq, tq, tk):
    pq, pk = [], []
    for qi in range(nq):
        for ki in range(((qi + 1) * tq - 1) // tk + 1):
            pq.append(qi)
            pk.append(ki)
    return np.asarray(pq, np.int32), np.asarray(pk, np.int32)


def _attention(qT, k, vT, n_heads, dk, dv, mask=None, pos=None, btab=None):
    bsz, _, seq = qT.shape
    dva = dv + ONES_ROWS
    vT = jnp.concatenate([vT.reshape(bsz, n_heads, dv, seq),
                          jnp.ones((bsz, n_heads, ONES_ROWS, seq), vT.dtype)], axis=2)
    vT = vT.reshape(bsz, n_heads * dva, seq)
    tq = min(TQ, seq)
    tk = min(TK, seq)
    nq, nk = seq // tq, seq // tk
    use_mask = mask is not None
    pq, pk = _causal_pairs(nq, tq, tk)

    in_specs = [
        pl.BlockSpec((1, n_heads * dk, tq), lambda b, s, pq, pk, *_: (b, 0, pq[s])),
        pl.BlockSpec((1, n_heads, tk, dk), lambda b, s, pq, pk, *_: (b, 0, pk[s], 0)),
        pl.BlockSpec((1, n_heads * dva, tk), lambda b, s, pq, pk, *_: (b, 0, pk[s])),
    ]
    args = [qT, k, vT]
    prefetch = [jnp.asarray(pq), jnp.asarray(pk)]
    if use_mask:
        prefetch += [jnp.min(pos.reshape(bsz, nq, tq), axis=-1), jnp.max(pos.reshape(bsz, nk, tk), axis=-1)]
        in_specs += [
            pl.BlockSpec((1, tk, tq), lambda b, s, pq, pk, *_: (b, pk[s], pq[s])),
            pl.BlockSpec((1, 1, tq), lambda b, s, pq, pk, *_: (b, 0, pq[s])),
            pl.BlockSpec((1, tk, 1), lambda b, s, pq, pk, *_: (b, pk[s], 0)),
            pl.BlockSpec(btab.shape, lambda b, s, pq, pk, *_: (0, 0)),
        ]
        args += [mask, pos.reshape(bsz, 1, seq), pos.reshape(bsz, seq, 1), btab]
    grid_spec = pltpu.PrefetchScalarGridSpec(
        num_scalar_prefetch=len(prefetch),
        grid=(bsz, len(pq)),
        in_specs=in_specs,
        out_specs=pl.BlockSpec((1, n_heads * dv, tq), lambda b, s, pq, pk, *_: (b, 0, pq[s])),
        scratch_shapes=[pltpu.VMEM((n_heads, tq), F32), pltpu.VMEM((n_heads * dva, tq), F32),
                        pltpu.VMEM((2, tk, tq), F32), pltpu.VMEM((2, tk, tq), BF16)],
    )
    return pl.pallas_call(
        functools.partial(_attn_kernel, n_heads=n_heads, dk=dk, dv=dv, use_mask=use_mask),
        grid_spec=grid_spec,
        out_shape=jax.ShapeDtypeStruct((bsz, n_heads * dv, seq), BF16),
        compiler_params=pltpu.CompilerParams(dimension_semantics=("parallel", "arbitrary"),
                                             vmem_limit_bytes=VMEM_LIMIT),
        name="dsa_attention" if use_mask else "mla_attention",
    )(*prefetch, *args)


def _layer_norm(z, g, b):
    mu = jnp.mean(z, axis=-1, keepdims=True)
    zc = z - mu
    var = jnp.mean(zc * zc, axis=-1, keepdims=True)
    return zc * lax.rsqrt(var + LN_EPS) * g + b


def _merge_kernel(x_ref, ya_ref, yb_ref, gate_ref, g1_ref, wa_ref, wb_ref, wo_ref, lg_ref, lb_ref,
                  o_ref, *, alpha):
    d = x_ref.shape[2]
    pa = jnp.dot(ya_ref[0], wa_ref[...], preferred_element_type=F32)
    pb = jnp.dot(yb_ref[0], wb_ref[...], preferred_element_type=F32)
    gate = gate_ref[0].astype(F32)
    merged = gate[:, :d] * pa + gate[:, d:] * pb
    y = jnp.dot(merged.astype(BF16), wo_ref[...], preferred_element_type=F32)
    o_ref[0] = _layer_norm(alpha * x_ref[0] + g1_ref[0] * y, lg_ref[...], lb_ref[...])


def _merge(x, ya, yb, gate, g1, wa, wb, wo, lg, lb, alpha):
    bsz, seq, d = x.shape
    tm = min(TM_PROJ, seq)
    tok = lambda width: pl.BlockSpec((1, tm, width), lambda b, i: (b, i, 0))
    full = lambda a: pl.BlockSpec(a.shape, lambda b, i: (0,) * a.ndim)
    return pl.pallas_call(
        functools.partial(_merge_kernel, alpha=alpha),
        grid=(bsz, seq // tm),
        in_specs=[tok(d), tok(ya.shape[2]), tok(yb.shape[2]), tok(2 * d),
                  pl.BlockSpec((1, 1, d), lambda b, i: (b, 0, 0)),
                  full(wa), full(wb), full(wo), full(lg), full(lb)],
        out_specs=tok(d),
        out_shape=jax.ShapeDtypeStruct((bsz, seq, d), F32),
        compiler_params=pltpu.CompilerParams(
            dimension_semantics=("parallel", "parallel"), vmem_limit_bytes=VMEM_LIMIT),
        name="merge_out_ln",
    )(x, ya, yb, gate, g1, wa, wb, wo, lg, lb)


def _ffn_kernel(x_ref, sc_ref, sh_ref, g2_ref, wup_ref, cw_ref, cb_ref, wdn_ref, lg_ref, lb_ref,
                o_ref, hist_ref, *, alpha, d_ff):
    i = pl.program_id(1)
    tm = x_ref.shape[1]
    x = x_ref[0]
    h = (x * (1.0 + sc_ref[0]) + sh_ref[0]).astype(BF16)
    u = jnp.dot(h, wup_ref[...], preferred_element_type=F32)

    @pl.when(i == 0)
    def _():
        hist_ref[...] = jnp.zeros(hist_ref.shape, F32)

    row = lax.broadcasted_iota(jnp.int32, u.shape, 0)
    prev = hist_ref[...]
    u1 = pltpu.roll(u, 1, axis=0)
    u1 = jnp.where(row == 0, prev[7:8, :], u1)
    u2 = pltpu.roll(u, 2, axis=0)
    u2 = jnp.where(row == 0, prev[6:7, :], jnp.where(row == 1, prev[7:8, :], u2))
    hist_ref[...] = u[tm - 8:, :]
    cw = cw_ref[...]
    uc = cw[0:1, :] * u2 + cw[1:2, :] * u1 + cw[2:3, :] * u + cb_ref[...]
    g = uc[:, :d_ff]
    val = uc[:, d_ff:]
    a = (g * _sigmoid(g) * val).astype(BF16)
    y = jnp.dot(a, wdn_ref[...], preferred_element_type=F32)
    o_ref[0] = _layer_norm(alpha * x + g2_ref[0] * y, lg_ref[...], lb_ref[...])


def _ffn(x, sc, sh, g2, wup, cw, cb, wdn, lg, lb, alpha):
    bsz, seq, d = x.shape
    d_ff = wdn.shape[0]
    tm = min(TM_FFN, seq)
    tok = pl.BlockSpec((1, tm, d), lambda b, i: (b, i, 0))
    per_b = pl.BlockSpec((1, 1, d), lambda b, i: (b, 0, 0))
    once = lambda a: pl.BlockSpec(a.shape, lambda b, i: (0,) * a.ndim, pipeline_mode=pl.Buffered(1))
    return pl.pallas_call(
        functools.partial(_ffn_kernel, alpha=alpha, d_ff=d_ff),
        grid=(bsz, seq // tm),
        in_specs=[tok, per_b, per_b, per_b, once(wup), once(cw), once(cb), once(wdn), once(lg), once(lb)],
        out_specs=tok,
        out_shape=jax.ShapeDtypeStruct((bsz, seq, d), F32),
        scratch_shapes=[pltpu.VMEM((8, 2 * d_ff), F32)],
        compiler_params=pltpu.CompilerParams(
            dimension_semantics=("parallel", "arbitrary"), vmem_limit_bytes=VMEM_LIMIT),
        name="conv_ffn_ln",
    )(x, sc, sh, g2, wup, cw, cb, wdn, lg, lb)


def _heads_major(a, n_heads):
    bsz, seq, width = a.shape
    return a.reshape(bsz, seq, n_heads, width // n_heads).transpose(0, 2, 1, 3)


def kernel(x, c, positions, rel_bias, w_ada, b_ada, w_in, q_norm_g, w_uq, kv_norm_g, w_ukv,
           w_branch_a, w_branch_b, w_out, ln1_g, ln1_b, w_up, conv_w, conv_b, w_down, ln2_g, ln2_b):
    bsz, seq, d = x.shape
    depth = w_ada.shape[0]
    alpha = (2 * depth) ** 0.25
    topk = min(TOPK_MAX, seq // 4)
    assert seq % TQ == 0 or seq < TQ

    mod = _ada_mod(c, w_ada, b_ada)
    pos3 = positions.reshape(bsz, seq, 1)
    invf = ROPE_THETA ** (-jnp.arange(ROPE_HALF, dtype=F32) * (2.0 / ROPE_DIM))
    invf = jnp.tile(invf, LANES // ROPE_HALF).reshape(1, LANES)
    btab = ((rel_bias[_BUCKET_TABLE] - rel_bias[NUM_BUCKETS - 1][None, :]) * LOG2E).T.astype(F32)

    for l in range(depth):
        sh1, sc1, g1, sh2, sc2, g2 = [mod[l, :, None, j * d:(j + 1) * d] for j in range(6)]
        qkva, iq, ik, iw, qb, kvb, krb, gate = _inproj(
            x, sc1, sh1, pos3, invf, _pack_w_in(w_in[l], d),
            q_norm_g[l].reshape(1, -1), _pack_w_uq(w_uq[l]),
            kv_norm_g[l].reshape(1, -1), _pack_w_ukv(w_ukv[l]))

        mask = _indexer(iq.transpose(0, 2, 1), iw.transpose(0, 2, 1), ik, topk)
        qaT = qkva[:, :, :A_WIDTH].transpose(0, 2, 1)
        ka = _heads_major(qkva[:, :, A_WIDTH:2 * A_WIDTH], A_HEADS)
        vaT = qkva[:, :, 2 * A_WIDTH:].transpose(0, 2, 1)
        yaT = _attention(qaT, ka, vaT, A_HEADS, A_HEAD_DIM, A_HEAD_DIM, mask=mask, pos=positions, btab=btab)

        n_nope = B_HEADS * NOPE_DIM
        n_r = B_HEADS * ROPE_HALF
        q_heads = jnp.concatenate([
            qb[:, :, :n_nope].reshape(bsz, seq, B_HEADS, NOPE_DIM),
            qb[:, :, n_nope:n_nope + n_r].reshape(bsz, seq, B_HEADS, ROPE_HALF),
            qb[:, :, n_nope + n_r:].reshape(bsz, seq, B_HEADS, ROPE_HALF)], axis=-1)
        qbT = q_heads.reshape(bsz, seq, B_HEADS * B_QK_DIM).transpose(0, 2, 1)
        kr = jnp.concatenate([krb[:, :, :ROPE_HALF], krb[:, :, LANES:LANES + ROPE_HALF]], axis=-1)
        k_heads = jnp.concatenate([
            kvb[:, :, :n_nope].reshape(bsz, seq, B_HEADS, NOPE_DIM),
            jnp.broadcast_to(kr[:, :, None, :], (bsz, seq, B_HEADS, ROPE_DIM))], axis=-1)
        kb = k_heads.transpose(0, 2, 1, 3)
        vbT = kvb[:, :, n_nope:].transpose(0, 2, 1)
        ybT = _attention(qbT, kb, vbT, B_HEADS, B_QK_DIM, V_DIM)

        x = _merge(x, yaT.transpose(0, 2, 1), ybT.transpose(0, 2, 1), gate, g1,
                   w_branch_a[l].astype(BF16), w_branch_b[l].astype(BF16), w_out[l].astype(BF16),
                   ln1_g[l].reshape(1, d), ln1_b[l].reshape(1, d), alpha)
        x = _ffn(x, sc2, sh2, g2, w_up[l].astype(BF16), conv_w[l], conv_b[l].reshape(1, -1),
                 w_down[l].astype(BF16), ln2_g[l].reshape(1, d), ln2_b[l].reshape(1, d), alpha)
    return x
```

```python
import functools
import math

import numpy as np
import jax
import jax.numpy as jnp
from jax import lax
from jax.experimental import pallas as pl
from jax.experimental.pallas import tpu as pltpu

F32 = jnp.float32
BF16 = jnp.bfloat16

A_HEADS = 8
A_HEAD_DIM = 64
IDX_HEADS = 8
IDX_DIM = 64
TOPK_MAX = 256
B_HEADS = 8
Q_RANK = 256
KV_RANK = 128
NOPE_DIM = 64
ROPE_DIM = 32
ROPE_HALF = ROPE_DIM // 2
V_DIM = 64
ROPE_THETA = 10000.0
NUM_BUCKETS = 32
MAX_DISTANCE = 128
CONV_WIDTH = 3
LN_EPS = 1e-5
RMS_EPS = 1e-6
A_WIDTH = A_HEADS * A_HEAD_DIM
B_WIDTH = B_HEADS * V_DIM
B_QK_DIM = NOPE_DIM + ROPE_DIM

LANES = 128
V7X_VMEM_BYTES = 64 * 1024 * 1024
VMEM_LIMIT = 52 * 1024 * 1024

NEG = -1e30
LOG2E = math.log2(math.e)

TM_PROJ = 512
TM_FFN = 256
TQ = 256
TK = 1024
IDX_CHUNK = 256
IDX_SCORE_CHUNK = 256
IDX_SCORE_GROUP = 4
IDX_COUNT_GROUP = 4
ATTN_SUB = 512
ATTN_CHUNK = 128
ONES_ROWS = 16
ATTN_P_BUFFERS = 2
ATTN_S_BUFFERS = 4


def _t5_bucket_table():
    n = np.arange(MAX_DISTANCE)
    max_exact = NUM_BUCKETS // 2
    out = []
    for dt in (np.float32, np.float64):
        ratio = np.log(np.maximum(n, 1).astype(dt) / dt(max_exact)) / dt(math.log(MAX_DISTANCE / max_exact))
        large = max_exact + (ratio * dt(NUM_BUCKETS - max_exact)).astype(np.int32)
        large = np.minimum(large, NUM_BUCKETS - 1)
        out.append(np.where(n < max_exact, n, large))
    assert (out[0] == out[1]).all()
    assert out[0][-1] == NUM_BUCKETS - 1
    return out[0].astype(np.int32)


_BUCKET_TABLE = _t5_bucket_table()


def _ada_kernel(c_ref, w_ref, b_ref, o_ref):
    c = c_ref[...]
    s = c * (1.0 / (1.0 + jnp.exp(-c)))
    o_ref[0] = jnp.dot(s, w_ref[0], preferred_element_type=F32) + b_ref[0]


def _ada_mod(c, w_ada, b_ada):
    depth, d, d6 = w_ada.shape
    bsz = c.shape[0]
    n_chunks = d6 // d
    return pl.pallas_call(
        _ada_kernel,
        grid=(depth, n_chunks),
        in_specs=[
            pl.BlockSpec((bsz, d), lambda l, j: (0, 0)),
            pl.BlockSpec((1, d, d), lambda l, j: (l, 0, j)),
            pl.BlockSpec((1, 1, d), lambda l, j: (l, 0, j)),
        ],
        out_specs=pl.BlockSpec((1, bsz, d), lambda l, j: (l, 0, j)),
        out_shape=jax.ShapeDtypeStruct((depth, bsz, d6), F32),
        name="ada_mod",
    )(c, w_ada, b_ada.reshape(depth, 1, d6))


_C_QKV = 0
_C_IQ = _C_QKV + 3 * A_WIDTH
_C_IK = _C_IQ + IDX_HEADS * IDX_DIM
_C_IW = _C_IK + LANES
_C_CQ = _C_IW + LANES
_C_CKV = _C_CQ + Q_RANK
_C_KR1 = _C_CKV + KV_RANK
_C_KR2 = _C_KR1 + LANES
_C_GATE = _C_KR2 + LANES


def _pack_w_in(w_in, d):
    sizes = (A_WIDTH, A_WIDTH, A_WIDTH, IDX_HEADS * IDX_DIM, IDX_DIM, IDX_HEADS,
             Q_RANK, KV_RANK, ROPE_DIM, d, d)
    offs = np.cumsum((0,) + sizes)
    seg = {name: w_in[:, offs[i]:offs[i + 1]] for i, name in enumerate(
        ("qa", "ka", "va", "iq", "ik", "iw", "cq", "ckv", "kr", "ga", "gb"))}

    def pad(w, width):
        return jnp.pad(w, ((0, 0), (0, width - w.shape[1])))

    parts = [seg["qa"], seg["ka"], seg["va"], seg["iq"],
             pad(seg["ik"], LANES), pad(seg["iw"], LANES),
             seg["cq"], seg["ckv"],
             pad(seg["kr"][:, :ROPE_HALF], LANES), pad(seg["kr"][:, ROPE_HALF:], LANES),
             seg["ga"], seg["gb"]]
    return jnp.concatenate(parts, axis=1).astype(BF16)


def _pack_w_uq(w_uq):
    w = w_uq.reshape(Q_RANK, B_HEADS, B_QK_DIM)
    nope = w[:, :, :NOPE_DIM].reshape(Q_RANK, B_HEADS * NOPE_DIM)
    r1 = w[:, :, NOPE_DIM:NOPE_DIM + ROPE_HALF].reshape(Q_RANK, B_HEADS * ROPE_HALF)
    r2 = w[:, :, NOPE_DIM + ROPE_HALF:].reshape(Q_RANK, B_HEADS * ROPE_HALF)
    return jnp.concatenate([nope, r1, r2], axis=1).astype(BF16)


def _pack_w_ukv(w_ukv):
    w = w_ukv.reshape(KV_RANK, B_HEADS, NOPE_DIM + V_DIM)
    kn = w[:, :, :NOPE_DIM].reshape(KV_RANK, B_HEADS * NOPE_DIM)
    v = w[:, :, NOPE_DIM:].reshape(KV_RANK, B_HEADS * V_DIM)
    return jnp.concatenate([kn, v], axis=1).astype(BF16)


def _sigmoid(x):
    return 1.0 / (1.0 + jnp.exp(-x))


def _rms(x, g):
    return x * lax.rsqrt(jnp.mean(x * x, axis=-1, keepdims=True) + RMS_EPS) * g


def _inproj_kernel(x_ref, sc_ref, sh_ref, pos_ref, invf_ref, w_ref, qg_ref, wuq_ref, kvg_ref, wukv_ref,
                   o_qkva, o_iq, o_ik, o_iw, o_qb, o_kvb, o_krb, o_gate, *, d_model):
    h = (x_ref[0] * (1.0 + sc_ref[0]) + sh_ref[0]).astype(BF16)

    def proj(c0, width):
        return jnp.dot(h, w_ref[:, c0:c0 + width], preferred_element_type=F32)

    a_scale = A_HEAD_DIM ** -0.5 * LOG2E
    o_qkva[0, :, 0:A_WIDTH] = (proj(_C_QKV, A_WIDTH) * a_scale).astype(BF16)
    o_qkva[0, :, A_WIDTH:3 * A_WIDTH] = proj(_C_QKV + A_WIDTH, 2 * A_WIDTH).astype(BF16)
    o_iq[0] = proj(_C_IQ, IDX_HEADS * IDX_DIM).astype(BF16)
    o_ik[0] = proj(_C_IK, LANES)[:, :IDX_DIM].astype(BF16)
    o_iw[0] = proj(_C_IW, LANES)[:, :IDX_HEADS] * (IDX_DIM ** -0.5 * IDX_HEADS ** -0.5)

    ang = pos_ref[0].astype(F32) * invf_ref[...]
    cos, sin = jnp.cos(ang), jnp.sin(ang)

    b_scale = B_QK_DIM ** -0.5 * LOG2E
    cq = _rms(proj(_C_CQ, Q_RANK), qg_ref[...]).astype(BF16)
    q = jnp.dot(cq, wuq_ref[...], preferred_element_type=F32)
    n_nope = B_HEADS * NOPE_DIM
    n_r = B_HEADS * ROPE_HALF
    x1 = q[:, n_nope:n_nope + n_r]
    x2 = q[:, n_nope + n_r:]
    o_qb[0, :, 0:n_nope] = (q[:, :n_nope] * b_scale).astype(BF16)
    o_qb[0, :, n_nope:n_nope + n_r] = ((x1 * cos - x2 * sin) * b_scale).astype(BF16)
    o_qb[0, :, n_nope + n_r:] = ((x1 * sin + x2 * cos) * b_scale).astype(BF16)

    ckv = _rms(proj(_C_CKV, KV_RANK), kvg_ref[...]).astype(BF16)
    o_kvb[0] = jnp.dot(ckv, wukv_ref[...], preferred_element_type=F32).astype(BF16)

    k1 = proj(_C_KR1, LANES)
    k2 = proj(_C_KR2, LANES)
    o_krb[0, :, 0:LANES] = (k1 * cos - k2 * sin).astype(BF16)
    o_krb[0, :, LANES:] = (k1 * sin + k2 * cos).astype(BF16)

    o_gate[0] = _sigmoid(proj(_C_GATE, 2 * d_model)).astype(BF16)


def _inproj(x, sc, sh, pos3, invf, w_pack, qg, wuq, kvg, wukv):
    bsz, seq, d = x.shape
    tm = min(TM_PROJ, seq)
    grid = (bsz, seq // tm)
    tok = lambda width: pl.BlockSpec((1, tm, width), lambda b, i: (b, i, 0))
    per_b = pl.BlockSpec((1, 1, d), lambda b, i: (b, 0, 0))
    full = lambda a: pl.BlockSpec(a.shape, lambda b, i: (0,) * a.ndim)
    out_widths = (3 * A_WIDTH, IDX_HEADS * IDX_DIM, IDX_DIM, IDX_HEADS,
                  B_HEADS * B_QK_DIM, B_HEADS * (NOPE_DIM + V_DIM), 2 * LANES, 2 * d)
    out_dtypes = (BF16, BF16, BF16, F32, BF16, BF16, BF16, BF16)
    return pl.pallas_call(
        functools.partial(_inproj_kernel, d_model=d),
        grid=grid,
        in_specs=[tok(d), per_b, per_b, tok(1), full(invf), full(w_pack), full(qg), full(wuq),
                  full(kvg), full(wukv)],
        out_specs=[tok(w) for w in out_widths],
        out_shape=[jax.ShapeDtypeStruct((bsz, seq, w), dt) for w, dt in zip(out_widths, out_dtypes)],
        compiler_params=pltpu.CompilerParams(
            dimension_semantics=("parallel", "parallel"), vmem_limit_bytes=VMEM_LIMIT),
        name="in_proj",
    )(x, sc, sh, pos3, invf, w_pack, qg, wuq, kvg, wukv)


def _ordered_key(x):
    b = lax.bitcast_convert_type(x, jnp.int32)
    return jnp.where(b < 0, b ^ jnp.int32(0x7FFFFFFF), b)


def _from_ordered_key(k):
    b = jnp.where(k < 0, k ^ jnp.int32(0x7FFFFFFF), k)
    return lax.bitcast_convert_type(b, F32)


def _indexer_kernel(iqT_ref, iwT_ref, ik_ref, o_ref, tcnt_ref, *, topk, seq):
    tq = o_ref.shape[2]
    ch = IDX_CHUNK
    sch = min(IDX_SCORE_CHUNK, ch)
    qi = pl.program_id(1)
    n_chunks = (qi * tq) // ch + tq // ch
    sub = 8
    kf = float(topk)

    def fold(v, op):
        return op(v.reshape(v.shape[0] // sub, sub, tq), axis=0)

    def count(pred):
        return fold(jnp.where(pred, 1.0, 0.0), jnp.sum)

    def chunk(c):
        k0 = pl.multiple_of(c * ch, ch)
        return k0, o_ref[0, pl.ds(k0, ch), :]

    def score_chunk(c, carry, diagonal):
        vmax, vmin, n_pos, n_nonneg = carry
        k0 = pl.multiple_of(c * sch, sch)
        ik = ik_ref[0, pl.ds(k0, sch), :]
        acc = jnp.zeros((sch, tq), F32)
        for hd in range(IDX_HEADS):
            y = jnp.dot(ik, iqT_ref[0, hd * IDX_DIM:(hd + 1) * IDX_DIM, :],
                        preferred_element_type=F32)
            acc = acc + jnp.maximum(y, 0.0) * iwT_ref[0, hd:hd + 1, :]
        if diagonal:
            key_idx = k0 + lax.broadcasted_iota(jnp.int32, (sch, tq), 0)
            q_idx = qi * tq + lax.broadcasted_iota(jnp.int32, (sch, tq), 1)
            causal = key_idx <= q_idx
            acc = jnp.where(causal, acc, -jnp.inf)
            vmin = jnp.minimum(vmin, fold(jnp.where(causal, acc, jnp.inf), jnp.min))
        else:
            vmin = jnp.minimum(vmin, fold(acc, jnp.min))
        o_ref[0, pl.ds(k0, sch), :] = acc
        vmax = jnp.maximum(vmax, fold(acc, jnp.max))
        n_pos = n_pos + count(acc > 0.0)
        n_nonneg = n_nonneg + count(acc >= 0.0)
        return vmax, vmin, n_pos, n_nonneg

    n_score = n_chunks * (ch // sch)
    n_diag = tq // sch
    n_full = n_score - n_diag
    sg = IDX_SCORE_GROUP

    def score_group(g, carry):
        for i in range(sg):
            carry = score_chunk(g * sg + i, carry, diagonal=False)
        return carry

    stats = (jnp.full((sub, tq), -jnp.inf, F32), jnp.full((sub, tq), jnp.inf, F32),
             jnp.zeros((sub, tq), F32), jnp.zeros((sub, tq), F32))
    stats = lax.fori_loop(0, n_full // sg, score_group, stats)
    stats = lax.fori_loop((n_full // sg) * sg, n_full, functools.partial(score_chunk, diagonal=False), stats)
    stats = lax.fori_loop(n_full, n_score, functools.partial(score_chunk, diagonal=True), stats)
    row_max = jnp.max(stats[0], axis=0, keepdims=True)
    row_min = jnp.min(stats[1], axis=0, keepdims=True)
    n_pos = jnp.sum(stats[2], axis=0, keepdims=True)
    n_nonneg = jnp.sum(stats[3], axis=0, keepdims=True)

    group = math.gcd(seq // ch, IDX_COUNT_GROUP)
    n_steps = (n_chunks + group - 1) // group

    def pad_chunk(c, carry):
        o_ref[0, pl.ds(pl.multiple_of(c * ch, ch), ch), :] = jnp.full((ch, tq), -jnp.inf, F32)
        return carry
    lax.fori_loop(n_chunks, n_steps * group, pad_chunk, 0)

    def count_ge(thr):
        def body(c, parts):
            return tuple(p + count(chunk(c * group + i)[1] >= thr) for i, p in enumerate(parts))
        parts = lax.fori_loop(0, n_steps, body, (jnp.zeros((sub, tq), F32),) * group)
        return jnp.sum(functools.reduce(jnp.add, parts), axis=0, keepdims=True)

    n_valid = (qi * tq + lax.broadcasted_iota(jnp.int32, (1, tq), 1) + 1).astype(F32)

    keep_all = n_valid <= kf
    zero_tie = jnp.logical_not(keep_all) & (n_pos < kf) & (n_nonneg >= kf)
    positive = n_pos >= kf
    lo_key0 = jnp.where(positive, 1, _ordered_key(row_min))
    hi_key0 = jnp.where(positive, _ordered_key(row_max) + 1, -2)
    done0 = jnp.where(keep_all | zero_tie, 1, 0).astype(jnp.int32)
    thr0 = jnp.where(keep_all, -jnp.inf, jnp.where(zero_tie, 0.0, row_min))
    tie0 = jnp.where(zero_tie, 1, 0).astype(jnp.int32)

    def n_active(done):
        return jnp.sum(jnp.where(done > 0, 0.0, 1.0))

    def cond(st):
        return st[0] > 0.0

    def body(st):
        _, it, lo_key, hi_key, thr, done, tie = st
        lo = _from_ordered_key(lo_key)
        hi = _from_ordered_key(hi_key)
        mid_val_key = _ordered_key(lo * 0.5 + hi * 0.5)
        mid_bit_key = (lo_key & hi_key) + ((lo_key ^ hi_key) >> 1)
        use_val = (it < 4) & (jnp.abs(hi) < jnp.inf) & (jnp.abs(lo) < jnp.inf)
        mid_key = jnp.where(use_val, mid_val_key, mid_bit_key)
        mid_key = jnp.minimum(jnp.maximum(mid_key, lo_key + 1), hi_key - 1)
        mid = _from_ordered_key(mid_key)
        cnt = count_ge(mid)
        active = done == 0
        ge = cnt >= kf
        lo_key = jnp.where(active & ge, mid_key, lo_key)
        hi_key = jnp.where(active & jnp.logical_not(ge), mid_key, hi_key)
        hit = active & (cnt == kf)
        stuck = active & jnp.logical_not(hit) & (hi_key - 1 <= lo_key)
        thr = jnp.where(hit, mid, jnp.where(stuck, _from_ordered_key(lo_key), thr))
        tie = jnp.where(stuck, 1, tie)
        done = jnp.where(hit | stuck, 1, done)
        return n_active(done), it + 1, lo_key, hi_key, thr, done, tie

    st = lax.while_loop(cond, body, (n_active(done0), jnp.int32(0), lo_key0, hi_key0, thr0, done0, tie0))
    thr, tie = st[4], st[6] > 0
    n_tie = jnp.sum(jnp.where(tie, 1.0, 0.0))

    def write_mask(k0, sel):
        o_ref[0, pl.ds(k0, ch), :] = jnp.where(sel, 0.0, NEG)

    def grouped(per_chunk):
        def body(g, carry):
            for i in range(group):
                carry = per_chunk(g * group + i, carry)
            return carry
        return body

    f32_info = jnp.finfo(F32)
    cut_incl = jnp.where(thr == -jnp.inf, float(f32_info.min), thr)
    above = _from_ordered_key(_ordered_key(thr) + 1)
    cut_excl = jnp.where((above >= 0.0) & (above < float(f32_info.tiny)), float(f32_info.tiny), above)

    def mask_plain():
        def body(c, carry):
            k0, v = chunk(c)
            write_mask(k0, v >= cut_incl)
            return carry
        lax.fori_loop(0, n_steps, grouped(body), 0)

    def mask_ties():
        def count_body(c, n_gt):
            _, v = chunk(c)
            tcnt_ref[c] = count(v == thr)
            return n_gt + count(v > thr)
        n_gt = lax.fori_loop(0, n_steps, grouped(count_body), jnp.zeros((sub, tq), F32))
        n_gt = jnp.sum(n_gt, axis=0, keepdims=True)
        need = jnp.where(tie, kf - n_gt, float(seq))

        def quota_body(c, carry):
            before, split = carry
            t = jnp.sum(tcnt_ref[c], axis=0, keepdims=True)
            keep = jnp.clip(need - before, 0.0, t)
            tcnt_ref[c, 0:1, :] = keep
            split = jnp.where((keep > 0.0) & (keep < t), c.astype(F32), split)
            return before + t, split
        _, split0 = lax.fori_loop(0, n_steps * group, quota_body,
                                  (jnp.zeros((1, tq), F32), jnp.full((1, tq), -1.0, F32)))

        def split_body(st):
            c_f, split = st
            c = c_f.astype(jnp.int32)
            k0, v = chunk(c)
            is_tie = v == thr
            r = lax.broadcasted_iota(jnp.int32, (ch, ch), 0)
            s = lax.broadcasted_iota(jnp.int32, (ch, ch), 1)
            lower = jnp.where(s < r, 1.0, 0.0).astype(BF16)
            rank = jnp.dot(lower, jnp.where(is_tie, 1.0, 0.0).astype(BF16), preferred_element_type=F32)
            o_ref[0, pl.ds(k0, ch), :] = jnp.where(is_tie & (rank >= tcnt_ref[c, 0:1, :]), -jnp.inf, v)
            split = jnp.where(split == c_f, -1.0, split)
            return jnp.max(split), split
        lax.while_loop(lambda st: st[0] >= 0.0, split_body, (jnp.max(split0), split0))

        def body(c, carry):
            k0, v = chunk(c)
            keep_ties = tcnt_ref[c, 0:1, :] > 0.0
            write_mask(k0, v >= jnp.where(keep_ties, cut_incl, cut_excl))
            return carry
        lax.fori_loop(0, n_steps, grouped(body), 0)

    lax.cond(n_tie > 0.0, mask_ties, mask_plain)

    def fill_chunk(c, carry):
        k0 = pl.multiple_of(c * ch, ch)
        o_ref[0, pl.ds(k0, ch), :] = jnp.full((ch, tq), NEG, F32)
        return carry

    lax.fori_loop(n_chunks, seq // ch, fill_chunk, 0)


def _indexer(iqT, iwT, ik, topk):
    bsz, _, seq = iqT.shape
    tq = min(TQ, seq)
    return pl.pallas_call(
        functools.partial(_indexer_kernel, topk=topk, seq=seq),
        grid=(bsz, seq // tq),
        in_specs=[
            pl.BlockSpec((1, IDX_HEADS * IDX_DIM, tq), lambda b, i: (b, 0, i)),
            pl.BlockSpec((1, IDX_HEADS, tq), lambda b, i: (b, 0, i)),
            pl.BlockSpec((1, seq, IDX_DIM), lambda b, i: (b, 0, 0)),
        ],
        out_specs=pl.BlockSpec((1, seq, tq), lambda b, i: (b, 0, i)),
        out_shape=jax.ShapeDtypeStruct((bsz, seq, seq), F32),
        scratch_shapes=[pltpu.VMEM((max(seq // IDX_CHUNK, 1), 8, tq), F32)],
        compiler_params=pltpu.CompilerParams(
            dimension_semantics=("parallel", "parallel"), vmem_limit_bytes=VMEM_LIMIT),
        name="dsa_indexer",
    )(iqT, iwT, ik)


def _attn_kernel(*refs, n_heads, dk, dv, use_mask):
    if use_mask:
        (pq_ref, pk_ref, qmin_ref, kmax_ref, qT_ref, k_ref, vT_ref, mask_ref, qpos_ref, kpos_ref, btab_ref,
         o_ref, m_sc, acc_sc, s_buf, p_buf) = refs
    else:
        pq_ref, pk_ref, qT_ref, k_ref, vT_ref, o_ref, m_sc, acc_sc, s_buf, p_buf = refs
    tq = qT_ref.shape[2]
    tk = k_ref.shape[2]
    b = pl.program_id(0)
    pair = pl.program_id(1)
    qi = pq_ref[pair]
    ki = pk_ref[pair]
    last_k = ((qi + 1) * tq - 1) // tk

    @pl.when(ki == 0)
    def _init():
        m_sc[...] = jnp.full(m_sc.shape, NEG, F32)
        acc_sc[...] = jnp.zeros(acc_sc.shape, F32)

    ts = s_buf.shape[1]
    ck = min(ATTN_CHUNK, ts)
    n_chunks = ts // ck
    dva = vT_ref.shape[1] // n_heads
    rows = lambda g: slice(g * ck, (g + 1) * ck)

    def step(extra):
        n_buf = s_buf.shape[0]
        n_pbuf = p_buf.shape[0]
        ahead = n_buf - 1
        units = [(j, h) for j in range(tk // ts) for h in range(n_heads)]

        def logits_to_buf(i):
            j, h = units[i]
            part = None
            for c in range(n_chunks):
                g = j * n_chunks + c
                s = jnp.dot(k_ref[0, h, rows(g), :], qT_ref[0, h * dk:(h + 1) * dk, :],
                            preferred_element_type=F32)
                s = extra(h, g, s)
                s_buf[i % n_buf, c * ck:(c + 1) * ck, :] = s
                part = s if part is None else jnp.maximum(part, s)
            return jnp.max(part, axis=0, keepdims=True)

        tile_max = {i: logits_to_buf(i) for i in range(min(ahead, len(units)))}
        for i, (j, h) in enumerate(units):
            m_prev = m_sc[h:h + 1, :]
            m_new = jnp.maximum(m_prev, tile_max.pop(i))
            alpha = jnp.exp2(m_prev - m_new)
            if i + ahead < len(units):
                tile_max[i + ahead] = logits_to_buf(i + ahead)
            p_buf[i % n_pbuf] = jnp.exp2(s_buf[i % n_buf] - m_new).astype(BF16)
            pv = jnp.dot(vT_ref[0, h * dva:(h + 1) * dva, j * ts:(j + 1) * ts], p_buf[i % n_pbuf],
                         preferred_element_type=F32)
            acc_sc[h * dva:(h + 1) * dva, :] = acc_sc[h * dva:(h + 1) * dva, :] * alpha + pv
            m_sc[h:h + 1, :] = m_new

    if use_mask:
        near = qmin_ref[b, qi] - kmax_ref[b, ki] < MAX_DISTANCE

        @pl.when(jnp.logical_not(near))
        def _far():
            step(lambda h, c, s: s + mask_ref[0, rows(c), :])

        @pl.when(near)
        def _near():
            def extra(h, c, s):
                rel = qpos_ref[0] - kpos_ref[0, rows(c), :]
                n = jnp.clip(rel, 0, MAX_DISTANCE - 1)
                tab = jnp.broadcast_to(btab_ref[h:h + 1, :], (ck, LANES))
                parts = [jnp.take_along_axis(tab, n[:, j * LANES:(j + 1) * LANES], axis=1)
                         for j in range(tq // LANES)]
                return s + mask_ref[0, rows(c), :] + jnp.concatenate(parts, axis=1)
            step(extra)
    else:
        diag = (ki + 1) * tk > qi * tq

        @pl.when(jnp.logical_not(diag))
        def _full():
            step(lambda h, c, s: s)

        @pl.when(diag)
        def _diag():
            def extra(h, c, s):
                key_idx = ki * tk + c * ck + lax.broadcasted_iota(jnp.int32, (ck, tq), 0)
                q_idx = qi * tq + lax.broadcasted_iota(jnp.int32, (ck, tq), 1)
                return jnp.where(key_idx <= q_idx, s, NEG)
            step(extra)

    @pl.when(ki == last_k)
    def _fin():
        for h in range(n_heads):
            inv = 1.0 / acc_sc[h * dva + dv:h * dva + dv + 1, :]
            o_ref[0, h * dv:(h + 1) * dv, :] = (acc_sc[h * dva:h * dva + dv, :] * inv).astype(o_ref.dtype)


def _causal_pairs(nq, tq, tk):
    pq, pk = [], []
    for qi in range(nq):
        for ki in range(((qi + 1) * tq - 1) // tk + 1):
            pq.append(qi)
            pk.append(ki)
    return np.asarray(pq, np.int32), np.asarray(pk, np.int32)


def _attention(qT, k, vT, n_heads, dk, dv, mask=None, pos=None, btab=None):
    bsz, _, seq = qT.shape
    dva = dv + ONES_ROWS
    vT = jnp.concatenate([vT.reshape(bsz, n_heads, dv, seq),
                          jnp.ones((bsz, n_heads, ONES_ROWS, seq), vT.dtype)], axis=2)
    vT = vT.reshape(bsz, n_heads * dva, seq)
    tq = min(TQ, seq)
    tk = min(TK, seq)
    ts = min(ATTN_SUB, tk)
    nq, nk = seq // tq, seq // tk
    use_mask = mask is not None
    pq, pk = _causal_pairs(nq, tq, tk)

    in_specs = [
        pl.BlockSpec((1, n_heads * dk, tq), lambda b, s, pq, pk, *_: (b, 0, pq[s])),
        pl.BlockSpec((1, n_heads, tk, dk), lambda b, s, pq, pk, *_: (b, 0, pk[s], 0)),
        pl.BlockSpec((1, n_heads * dva, tk), lambda b, s, pq, pk, *_: (b, 0, pk[s])),
    ]
    args = [qT, k, vT]
    prefetch = [jnp.asarray(pq), jnp.asarray(pk)]
    if use_mask:
        prefetch += [jnp.min(pos.reshape(bsz, nq, tq), axis=-1), jnp.max(pos.reshape(bsz, nk, tk), axis=-1)]
        in_specs += [
            pl.BlockSpec((1, tk, tq), lambda b, s, pq, pk, *_: (b, pk[s], pq[s])),
            pl.BlockSpec((1, 1, tq), lambda b, s, pq, pk, *_: (b, 0, pq[s])),
            pl.BlockSpec((1, tk, 1), lambda b, s, pq, pk, *_: (b, pk[s], 0)),
            pl.BlockSpec(btab.shape, lambda b, s, pq, pk, *_: (0, 0)),
        ]
        args += [mask, pos.reshape(bsz, 1, seq), pos.reshape(bsz, seq, 1), btab]
    grid_spec = pltpu.PrefetchScalarGridSpec(
        num_scalar_prefetch=len(prefetch),
        grid=(bsz, len(pq)),
        in_specs=in_specs,
        out_specs=pl.BlockSpec((1, n_heads * dv, tq), lambda b, s, pq, pk, *_: (b, 0, pq[s])),
        scratch_shapes=[pltpu.VMEM((n_heads, tq), F32), pltpu.VMEM((n_heads * dva, tq), F32),
                        pltpu.VMEM((ATTN_S_BUFFERS, ts, tq), F32), pltpu.VMEM((ATTN_P_BUFFERS, ts, tq), BF16)],
    )
    return pl.pallas_call(
        functools.partial(_attn_kernel, n_heads=n_heads, dk=dk, dv=dv, use_mask=use_mask),
        grid_spec=grid_spec,
        out_shape=jax.ShapeDtypeStruct((bsz, n_heads * dv, seq), BF16),
        compiler_params=pltpu.CompilerParams(dimension_semantics=("parallel", "arbitrary"),
                                             vmem_limit_bytes=VMEM_LIMIT),
        name="dsa_attention" if use_mask else "mla_attention",
    )(*prefetch, *args)


def _layer_norm(z, g, b):
    mu = jnp.mean(z, axis=-1, keepdims=True)
    zc = z - mu
    var = jnp.mean(zc * zc, axis=-1, keepdims=True)
    return zc * lax.rsqrt(var + LN_EPS) * g + b


def _merge_kernel(x_ref, ya_ref, yb_ref, gate_ref, g1_ref, wa_ref, wb_ref, wo_ref, lg_ref, lb_ref,
                  o_ref, *, alpha):
    d = x_ref.shape[2]
    pa = jnp.dot(ya_ref[0], wa_ref[...], preferred_element_type=F32)
    pb = jnp.dot(yb_ref[0], wb_ref[...], preferred_element_type=F32)
    gate = gate_ref[0].astype(F32)
    merged = gate[:, :d] * pa + gate[:, d:] * pb
    y = jnp.dot(merged.astype(BF16), wo_ref[...], preferred_element_type=F32)
    o_ref[0] = _layer_norm(alpha * x_ref[0] + g1_ref[0] * y, lg_ref[...], lb_ref[...])


def _merge(x, ya, yb, gate, g1, wa, wb, wo, lg, lb, alpha):
    bsz, seq, d = x.shape
    tm = min(TM_PROJ, seq)
    tok = lambda width: pl.BlockSpec((1, tm, width), lambda b, i: (b, i, 0))
    full = lambda a: pl.BlockSpec(a.shape, lambda b, i: (0,) * a.ndim)
    return pl.pallas_call(
        functools.partial(_merge_kernel, alpha=alpha),
        grid=(bsz, seq // tm),
        in_specs=[tok(d), tok(ya.shape[2]), tok(yb.shape[2]), tok(2 * d),
                  pl.BlockSpec((1, 1, d), lambda b, i: (b, 0, 0)),
                  full(wa), full(wb), full(wo), full(lg), full(lb)],
        out_specs=tok(d),
        out_shape=jax.ShapeDtypeStruct((bsz, seq, d), F32),
        compiler_params=pltpu.CompilerParams(
            dimension_semantics=("parallel", "parallel"), vmem_limit_bytes=VMEM_LIMIT),
        name="merge_out_ln",
    )(x, ya, yb, gate, g1, wa, wb, wo, lg, lb)


def _ffn_kernel(x_ref, sc_ref, sh_ref, g2_ref, wup_ref, cw_ref, cb_ref, wdn_ref, lg_ref, lb_ref,
                o_ref, hist_ref, *, alpha, d_ff):
    i = pl.program_id(1)
    tm = x_ref.shape[1]
    x = x_ref[0]
    h = (x * (1.0 + sc_ref[0]) + sh_ref[0]).astype(BF16)
    u = jnp.dot(h, wup_ref[...], preferred_element_type=F32)

    @pl.when(i == 0)
    def _():
        hist_ref[...] = jnp.zeros(hist_ref.shape, F32)

    row = lax.broadcasted_iota(jnp.int32, u.shape, 0)
    prev = hist_ref[...]
    u1 = pltpu.roll(u, 1, axis=0)
    u1 = jnp.where(row == 0, prev[7:8, :], u1)
    u2 = pltpu.roll(u, 2, axis=0)
    u2 = jnp.where(row == 0, prev[6:7, :], jnp.where(row == 1, prev[7:8, :], u2))
    hist_ref[...] = u[tm - 8:, :]
    cw = cw_ref[...]
    uc = cw[0:1, :] * u2 + cw[1:2, :] * u1 + cw[2:3, :] * u + cb_ref[...]
    g = uc[:, :d_ff]
    val = uc[:, d_ff:]
    a = (g * _sigmoid(g) * val).astype(BF16)
    y = jnp.dot(a, wdn_ref[...], preferred_element_type=F32)
    o_ref[0] = _layer_norm(alpha * x + g2_ref[0] * y, lg_ref[...], lb_ref[...])


def _ffn(x, sc, sh, g2, wup, cw, cb, wdn, lg, lb, alpha):
    bsz, seq, d = x.shape
    d_ff = wdn.shape[0]
    tm = min(TM_FFN, seq)
    tok = pl.BlockSpec((1, tm, d), lambda b, i: (b, i, 0))
    per_b = pl.BlockSpec((1, 1, d), lambda b, i: (b, 0, 0))
    once = lambda a: pl.BlockSpec(a.shape, lambda b, i: (0,) * a.ndim, pipeline_mode=pl.Buffered(1))
    return pl.pallas_call(
        functools.partial(_ffn_kernel, alpha=alpha, d_ff=d_ff),
        grid=(bsz, seq // tm),
        in_specs=[tok, per_b, per_b, per_b, once(wup), once(cw), once(cb), once(wdn), once(lg), once(lb)],
        out_specs=tok,
        out_shape=jax.ShapeDtypeStruct((bsz, seq, d), F32),
        scratch_shapes=[pltpu.VMEM((8, 2 * d_ff), F32)],
        compiler_params=pltpu.CompilerParams(
            dimension_semantics=("parallel", "arbitrary"), vmem_limit_bytes=VMEM_LIMIT),
        name="conv_ffn_ln",
    )(x, sc, sh, g2, wup, cw, cb, wdn, lg, lb)


def _heads_major(a, n_heads):
    bsz, seq, width = a.shape
    return a.reshape(bsz, seq, n_heads, width // n_heads).transpose(0, 2, 1, 3)


def kernel(x, c, positions, rel_bias, w_ada, b_ada, w_in, q_norm_g, w_uq, kv_norm_g, w_ukv,
           w_branch_a, w_branch_b, w_out, ln1_g, ln1_b, w_up, conv_w, conv_b, w_down, ln2_g, ln2_b):
    bsz, seq, d = x.shape
    depth = w_ada.shape[0]
    alpha = (2 * depth) ** 0.25
    topk = min(TOPK_MAX, seq // 4)
    assert seq % TQ == 0 or seq < TQ

    mod = _ada_mod(c, w_ada, b_ada)
    pos3 = positions.reshape(bsz, seq, 1)
    invf = ROPE_THETA ** (-jnp.arange(ROPE_HALF, dtype=F32) * (2.0 / ROPE_DIM))
    invf = jnp.tile(invf, LANES // ROPE_HALF).reshape(1, LANES)
    btab = ((rel_bias[_BUCKET_TABLE] - rel_bias[NUM_BUCKETS - 1][None, :]) * LOG2E).T.astype(F32)

    for l in range(depth):
        sh1, sc1, g1, sh2, sc2, g2 = [mod[l, :, None, j * d:(j + 1) * d] for j in range(6)]
        qkva, iq, ik, iw, qb, kvb, krb, gate = _inproj(
            x, sc1, sh1, pos3, invf, _pack_w_in(w_in[l], d),
            q_norm_g[l].reshape(1, -1), _pack_w_uq(w_uq[l]),
            kv_norm_g[l].reshape(1, -1), _pack_w_ukv(w_ukv[l]))

        mask = _indexer(iq.transpose(0, 2, 1), iw.transpose(0, 2, 1), ik, topk)
        qaT = qkva[:, :, :A_WIDTH].transpose(0, 2, 1)
        ka = _heads_major(qkva[:, :, A_WIDTH:2 * A_WIDTH], A_HEADS)
        vaT = qkva[:, :, 2 * A_WIDTH:].transpose(0, 2, 1)
        yaT = _attention(qaT, ka, vaT, A_HEADS, A_HEAD_DIM, A_HEAD_DIM, mask=mask, pos=positions, btab=btab)

        n_nope = B_HEADS * NOPE_DIM
        n_r = B_HEADS * ROPE_HALF
        q_heads = jnp.concatenate([
            qb[:, :, :n_nope].reshape(bsz, seq, B_HEADS, NOPE_DIM),
            qb[:, :, n_nope:n_nope + n_r].reshape(bsz, seq, B_HEADS, ROPE_HALF),
            qb[:, :, n_nope + n_r:].reshape(bsz, seq, B_HEADS, ROPE_HALF)], axis=-1)
        qbT = q_heads.reshape(bsz, seq, B_HEADS * B_QK_DIM).transpose(0, 2, 1)
        kr = jnp.concatenate([krb[:, :, :ROPE_HALF], krb[:, :, LANES:LANES + ROPE_HALF]], axis=-1)
        k_heads = jnp.concatenate([
            kvb[:, :, :n_nope].reshape(bsz, seq, B_HEADS, NOPE_DIM),
            jnp.broadcast_to(kr[:, :, None, :], (bsz, seq, B_HEADS, ROPE_DIM))], axis=-1)
        kb = k_heads.transpose(0, 2, 1, 3)
        vbT = kvb[:, :, n_nope:].transpose(0, 2, 1)
        ybT = _attention(qbT, kb, vbT, B_HEADS, B_QK_DIM, V_DIM)

        x = _merge(x, yaT.transpose(0, 2, 1), ybT.transpose(0, 2, 1), gate, g1,
                   w_branch_a[l].astype(BF16), w_branch_b[l].astype(BF16), w_out[l].astype(BF16),
                   ln1_g[l].reshape(1, d), ln1_b[l].reshape(1, d), alpha)
        x = _ffn(x, sc2, sh2, g2, w_up[l].astype(BF16), conv_w[l], conv_b[l].reshape(1, -1),
                 w_down[l].astype(BF16), ln2_g[l].reshape(1, d), ln2_b[l].reshape(1, d), alpha)
    return x
```

```python
import functools
import math

import numpy as np
import jax
import jax.numpy as jnp
from jax import lax
from jax.experimental import pallas as pl
from jax.experimental.pallas import tpu as pltpu

F32 = jnp.float32
BF16 = jnp.bfloat16

A_HEADS = 8
A_HEAD_DIM = 64
IDX_HEADS = 8
IDX_DIM = 64
TOPK_MAX = 256
B_HEADS = 8
Q_RANK = 256
KV_RANK = 128
NOPE_DIM = 64
ROPE_DIM = 32
ROPE_HALF = ROPE_DIM // 2
V_DIM = 64
ROPE_THETA = 10000.0
NUM_BUCKETS = 32
MAX_DISTANCE = 128
CONV_WIDTH = 3
LN_EPS = 1e-5
RMS_EPS = 1e-6
A_WIDTH = A_HEADS * A_HEAD_DIM
B_WIDTH = B_HEADS * V_DIM
B_QK_DIM = NOPE_DIM + ROPE_DIM

LANES = 128
V7X_VMEM_BYTES = 64 * 1024 * 1024
VMEM_LIMIT = 52 * 1024 * 1024

NEG = -1e30
LOG2E = math.log2(math.e)

TM_PROJ = 512
TM_FFN = 512
FFN_CHUNK = 1408
TQ = 256
TK = 1024
IDX_CHUNK = 256
IDX_SCORE_CHUNK = 256
IDX_SCORE_GROUP = 4
IDX_COUNT_GROUP = 4
ATTN_SUB = 512
ATTN_CHUNK = 128
ONES_ROWS = 16
ATTN_P_BUFFERS = 2
ATTN_S_BUFFERS = 4


def _t5_bucket_table():
    n = np.arange(MAX_DISTANCE)
    max_exact = NUM_BUCKETS // 2
    out = []
    for dt in (np.float32, np.float64):
        ratio = np.log(np.maximum(n, 1).astype(dt) / dt(max_exact)) / dt(math.log(MAX_DISTANCE / max_exact))
        large = max_exact + (ratio * dt(NUM_BUCKETS - max_exact)).astype(np.int32)
        large = np.minimum(large, NUM_BUCKETS - 1)
        out.append(np.where(n < max_exact, n, large))
    assert (out[0] == out[1]).all()
    assert out[0][-1] == NUM_BUCKETS - 1
    return out[0].astype(np.int32)


_BUCKET_TABLE = _t5_bucket_table()


def _ada_kernel(c_ref, w_ref, b_ref, o_ref):
    c = c_ref[...]
    s = c * (1.0 / (1.0 + jnp.exp(-c)))
    o_ref[0] = jnp.dot(s, w_ref[0], preferred_element_type=F32) + b_ref[0]


def _ada_mod(c, w_ada, b_ada):
    depth, d, d6 = w_ada.shape
    bsz = c.shape[0]
    n_chunks = d6 // d
    return pl.pallas_call(
        _ada_kernel,
        grid=(depth, n_chunks),
        in_specs=[
            pl.BlockSpec((bsz, d), lambda l, j: (0, 0)),
            pl.BlockSpec((1, d, d), lambda l, j: (l, 0, j)),
            pl.BlockSpec((1, 1, d), lambda l, j: (l, 0, j)),
        ],
        out_specs=pl.BlockSpec((1, bsz, d), lambda l, j: (l, 0, j)),
        out_shape=jax.ShapeDtypeStruct((depth, bsz, d6), F32),
        name="ada_mod",
    )(c, w_ada, b_ada.reshape(depth, 1, d6))


_C_QKV = 0
_C_IQ = _C_QKV + 3 * A_WIDTH
_C_IK = _C_IQ + IDX_HEADS * IDX_DIM
_C_IW = _C_IK + LANES
_C_CQ = _C_IW + LANES
_C_CKV = _C_CQ + Q_RANK
_C_KR1 = _C_CKV + KV_RANK
_C_KR2 = _C_KR1 + LANES
_C_GATE = _C_KR2 + LANES


def _pack_w_in(w_in, d):
    sizes = (A_WIDTH, A_WIDTH, A_WIDTH, IDX_HEADS * IDX_DIM, IDX_DIM, IDX_HEADS,
             Q_RANK, KV_RANK, ROPE_DIM, d, d)
    offs = np.cumsum((0,) + sizes)
    seg = {name: w_in[:, offs[i]:offs[i + 1]] for i, name in enumerate(
        ("qa", "ka", "va", "iq", "ik", "iw", "cq", "ckv", "kr", "ga", "gb"))}

    def pad(w, width):
        return jnp.pad(w, ((0, 0), (0, width - w.shape[1])))

    parts = [seg["qa"], seg["ka"], seg["va"], seg["iq"],
             pad(seg["ik"], LANES), pad(seg["iw"], LANES),
             seg["cq"], seg["ckv"],
             pad(seg["kr"][:, :ROPE_HALF], LANES), pad(seg["kr"][:, ROPE_HALF:], LANES),
             seg["ga"], seg["gb"]]
    return jnp.concatenate(parts, axis=1).astype(BF16)


def _pack_w_uq(w_uq):
    w = w_uq.reshape(Q_RANK, B_HEADS, B_QK_DIM)
    nope = w[:, :, :NOPE_DIM].reshape(Q_RANK, B_HEADS * NOPE_DIM)
    r1 = w[:, :, NOPE_DIM:NOPE_DIM + ROPE_HALF].reshape(Q_RANK, B_HEADS * ROPE_HALF)
    r2 = w[:, :, NOPE_DIM + ROPE_HALF:].reshape(Q_RANK, B_HEADS * ROPE_HALF)
    return jnp.concatenate([nope, r1, r2], axis=1).astype(BF16)


def _pack_w_ukv(w_ukv):
    w = w_ukv.reshape(KV_RANK, B_HEADS, NOPE_DIM + V_DIM)
    kn = w[:, :, :NOPE_DIM].reshape(KV_RANK, B_HEADS * NOPE_DIM)
    v = w[:, :, NOPE_DIM:].reshape(KV_RANK, B_HEADS * V_DIM)
    return jnp.concatenate([kn, v], axis=1).astype(BF16)


def _sigmoid(x):
    return 1.0 / (1.0 + jnp.exp(-x))


def _rms(x, g):
    return x * lax.rsqrt(jnp.mean(x * x, axis=-1, keepdims=True) + RMS_EPS) * g


def _inproj_kernel(x_ref, sc_ref, sh_ref, pos_ref, invf_ref, w_ref, qg_ref, wuq_ref, kvg_ref, wukv_ref,
                   o_qkva, o_iq, o_ik, o_iw, o_qb, o_kvb, o_krb, o_gate, *, d_model):
    h = (x_ref[0] * (1.0 + sc_ref[0]) + sh_ref[0]).astype(BF16)

    def proj(c0, width):
        return jnp.dot(h, w_ref[:, c0:c0 + width], preferred_element_type=F32)

    a_scale = A_HEAD_DIM ** -0.5 * LOG2E
    o_qkva[0, :, 0:A_WIDTH] = (proj(_C_QKV, A_WIDTH) * a_scale).astype(BF16)
    o_qkva[0, :, A_WIDTH:3 * A_WIDTH] = proj(_C_QKV + A_WIDTH, 2 * A_WIDTH).astype(BF16)
    o_iq[0] = proj(_C_IQ, IDX_HEADS * IDX_DIM).astype(BF16)
    o_ik[0] = proj(_C_IK, LANES)[:, :IDX_DIM].astype(BF16)
    o_iw[0] = proj(_C_IW, LANES)[:, :IDX_HEADS] * (IDX_DIM ** -0.5 * IDX_HEADS ** -0.5)

    ang = pos_ref[0].astype(F32) * invf_ref[...]
    cos, sin = jnp.cos(ang), jnp.sin(ang)

    b_scale = B_QK_DIM ** -0.5 * LOG2E
    cq = _rms(proj(_C_CQ, Q_RANK), qg_ref[...]).astype(BF16)
    q = jnp.dot(cq, wuq_ref[...], preferred_element_type=F32)
    n_nope = B_HEADS * NOPE_DIM
    n_r = B_HEADS * ROPE_HALF
    x1 = q[:, n_nope:n_nope + n_r]
    x2 = q[:, n_nope + n_r:]
    o_qb[0, :, 0:n_nope] = (q[:, :n_nope] * b_scale).astype(BF16)
    o_qb[0, :, n_nope:n_nope + n_r] = ((x1 * cos - x2 * sin) * b_scale).astype(BF16)
    o_qb[0, :, n_nope + n_r:] = ((x1 * sin + x2 * cos) * b_scale).astype(BF16)

    ckv = _rms(proj(_C_CKV, KV_RANK), kvg_ref[...]).astype(BF16)
    o_kvb[0] = jnp.dot(ckv, wukv_ref[...], preferred_element_type=F32).astype(BF16)

    k1 = proj(_C_KR1, LANES)
    k2 = proj(_C_KR2, LANES)
    o_krb[0, :, 0:LANES] = (k1 * cos - k2 * sin).astype(BF16)
    o_krb[0, :, LANES:] = (k1 * sin + k2 * cos).astype(BF16)

    o_gate[0] = _sigmoid(proj(_C_GATE, 2 * d_model)).astype(BF16)


def _inproj(x, sc, sh, pos3, invf, w_pack, qg, wuq, kvg, wukv):
    bsz, seq, d = x.shape
    tm = min(TM_PROJ, seq)
    grid = (bsz, seq // tm)
    tok = lambda width: pl.BlockSpec((1, tm, width), lambda b, i: (b, i, 0))
    per_b = pl.BlockSpec((1, 1, d), lambda b, i: (b, 0, 0))
    full = lambda a: pl.BlockSpec(a.shape, lambda b, i: (0,) * a.ndim)
    out_widths = (3 * A_WIDTH, IDX_HEADS * IDX_DIM, IDX_DIM, IDX_HEADS,
                  B_HEADS * B_QK_DIM, B_HEADS * (NOPE_DIM + V_DIM), 2 * LANES, 2 * d)
    out_dtypes = (BF16, BF16, BF16, F32, BF16, BF16, BF16, BF16)
    return pl.pallas_call(
        functools.partial(_inproj_kernel, d_model=d),
        grid=grid,
        in_specs=[tok(d), per_b, per_b, tok(1), full(invf), full(w_pack), full(qg), full(wuq),
                  full(kvg), full(wukv)],
        out_specs=[tok(w) for w in out_widths],
        out_shape=[jax.ShapeDtypeStruct((bsz, seq, w), dt) for w, dt in zip(out_widths, out_dtypes)],
        compiler_params=pltpu.CompilerParams(
            dimension_semantics=("parallel", "parallel"), vmem_limit_bytes=VMEM_LIMIT),
        name="in_proj",
    )(x, sc, sh, pos3, invf, w_pack, qg, wuq, kvg, wukv)


def _ordered_key(x):
    b = lax.bitcast_convert_type(x, jnp.int32)
    return jnp.where(b < 0, b ^ jnp.int32(0x7FFFFFFF), b)


def _from_ordered_key(k):
    b = jnp.where(k < 0, k ^ jnp.int32(0x7FFFFFFF), k)
    return lax.bitcast_convert_type(b, F32)


def _indexer_kernel(iqT_ref, iwT_ref, ik_ref, o_ref, tcnt_ref, *, topk, seq):
    tq = o_ref.shape[2]
    ch = IDX_CHUNK
    sch = min(IDX_SCORE_CHUNK, ch)
    qi = pl.program_id(1)
    n_chunks = (qi * tq) // ch + tq // ch
    sub = 8
    kf = float(topk)

    def fold(v, op):
        return op(v.reshape(v.shape[0] // sub, sub, tq), axis=0)

    def count(pred):
        return fold(jnp.where(pred, 1.0, 0.0), jnp.sum)

    def chunk(c):
        k0 = pl.multiple_of(c * ch, ch)
        return k0, o_ref[0, pl.ds(k0, ch), :]

    def score_chunk(c, carry, diagonal):
        vmax, vmin, n_pos, n_nonneg = carry
        k0 = pl.multiple_of(c * sch, sch)
        ik = ik_ref[0, pl.ds(k0, sch), :]
        acc = jnp.zeros((sch, tq), F32)
        for hd in range(IDX_HEADS):
            y = jnp.dot(ik, iqT_ref[0, hd * IDX_DIM:(hd + 1) * IDX_DIM, :],
                        preferred_element_type=F32)
            acc = acc + jnp.maximum(y, 0.0) * iwT_ref[0, hd:hd + 1, :]
        if diagonal:
            key_idx = k0 + lax.broadcasted_iota(jnp.int32, (sch, tq), 0)
            q_idx = qi * tq + lax.broadcasted_iota(jnp.int32, (sch, tq), 1)
            causal = key_idx <= q_idx
            acc = jnp.where(causal, acc, -jnp.inf)
            vmin = jnp.minimum(vmin, fold(jnp.where(causal, acc, jnp.inf), jnp.min))
        else:
            vmin = jnp.minimum(vmin, fold(acc, jnp.min))
        o_ref[0, pl.ds(k0, sch), :] = acc
        vmax = jnp.maximum(vmax, fold(acc, jnp.max))
        n_pos = n_pos + count(acc > 0.0)
        n_nonneg = n_nonneg + count(acc >= 0.0)
        return vmax, vmin, n_pos, n_nonneg

    n_score = n_chunks * (ch // sch)
    n_diag = tq // sch
    n_full = n_score - n_diag
    sg = IDX_SCORE_GROUP

    def score_group(g, carry):
        for i in range(sg):
            carry = score_chunk(g * sg + i, carry, diagonal=False)
        return carry

    stats = (jnp.full((sub, tq), -jnp.inf, F32), jnp.full((sub, tq), jnp.inf, F32),
             jnp.zeros((sub, tq), F32), jnp.zeros((sub, tq), F32))
    stats = lax.fori_loop(0, n_full // sg, score_group, stats)
    stats = lax.fori_loop((n_full // sg) * sg, n_full, functools.partial(score_chunk, diagonal=False), stats)
    stats = lax.fori_loop(n_full, n_score, functools.partial(score_chunk, diagonal=True), stats)
    row_max = jnp.max(stats[0], axis=0, keepdims=True)
    row_min = jnp.min(stats[1], axis=0, keepdims=True)
    n_pos = jnp.sum(stats[2], axis=0, keepdims=True)
    n_nonneg = jnp.sum(stats[3], axis=0, keepdims=True)

    group = math.gcd(seq // ch, IDX_COUNT_GROUP)
    n_steps = (n_chunks + group - 1) // group

    def pad_chunk(c, carry):
        o_ref[0, pl.ds(pl.multiple_of(c * ch, ch), ch), :] = jnp.full((ch, tq), -jnp.inf, F32)
        return carry
    lax.fori_loop(n_chunks, n_steps * group, pad_chunk, 0)

    def count_ge(thr):
        def body(c, parts):
            return tuple(p + count(chunk(c * group + i)[1] >= thr) for i, p in enumerate(parts))
        parts = lax.fori_loop(0, n_steps, body, (jnp.zeros((sub, tq), F32),) * group)
        return jnp.sum(functools.reduce(jnp.add, parts), axis=0, keepdims=True)

    n_valid = (qi * tq + lax.broadcasted_iota(jnp.int32, (1, tq), 1) + 1).astype(F32)

    keep_all = n_valid <= kf
    zero_tie = jnp.logical_not(keep_all) & (n_pos < kf) & (n_nonneg >= kf)
    positive = n_pos >= kf
    lo_key0 = jnp.where(positive, 1, _ordered_key(row_min))
    hi_key0 = jnp.where(positive, _ordered_key(row_max) + 1, -2)
    done0 = jnp.where(keep_all | zero_tie, 1, 0).astype(jnp.int32)
    thr0 = jnp.where(keep_all, -jnp.inf, jnp.where(zero_tie, 0.0, row_min))
    tie0 = jnp.where(zero_tie, 1, 0).astype(jnp.int32)

    def n_active(done):
        return jnp.sum(jnp.where(done > 0, 0.0, 1.0))

    def cond(st):
        return st[0] > 0.0

    def body(st):
        _, it, lo_key, hi_key, thr, done, tie = st
        lo = _from_ordered_key(lo_key)
        hi = _from_ordered_key(hi_key)
        mid_val_key = _ordered_key(lo * 0.5 + hi * 0.5)
        mid_bit_key = (lo_key & hi_key) + ((lo_key ^ hi_key) >> 1)
        use_val = (it < 4) & (jnp.abs(hi) < jnp.inf) & (jnp.abs(lo) < jnp.inf)
        mid_key = jnp.where(use_val, mid_val_key, mid_bit_key)
        mid_key = jnp.minimum(jnp.maximum(mid_key, lo_key + 1), hi_key - 1)
        mid = _from_ordered_key(mid_key)
        cnt = count_ge(mid)
        active = done == 0
        ge = cnt >= kf
        lo_key = jnp.where(active & ge, mid_key, lo_key)
        hi_key = jnp.where(active & jnp.logical_not(ge), mid_key, hi_key)
        hit = active & (cnt == kf)
        stuck = active & jnp.logical_not(hit) & (hi_key - 1 <= lo_key)
        thr = jnp.where(hit, mid, jnp.where(stuck, _from_ordered_key(lo_key), thr))
        tie = jnp.where(stuck, 1, tie)
        done = jnp.where(hit | stuck, 1, done)
        return n_active(done), it + 1, lo_key, hi_key, thr, done, tie

    st = lax.while_loop(cond, body, (n_active(done0), jnp.int32(0), lo_key0, hi_key0, thr0, done0, tie0))
    thr, tie = st[4], st[6] > 0
    n_tie = jnp.sum(jnp.where(tie, 1.0, 0.0))

    def write_mask(k0, sel):
        o_ref[0, pl.ds(k0, ch), :] = jnp.where(sel, 0.0, NEG)

    def grouped(per_chunk):
        def body(g, carry):
            for i in range(group):
                carry = per_chunk(g * group + i, carry)
            return carry
        return body

    f32_info = jnp.finfo(F32)
    cut_incl = jnp.where(thr == -jnp.inf, float(f32_info.min), thr)
    above = _from_ordered_key(_ordered_key(thr) + 1)
    cut_excl = jnp.where((above >= 0.0) & (above < float(f32_info.tiny)), float(f32_info.tiny), above)

    def mask_plain():
        def body(c, carry):
            k0, v = chunk(c)
            write_mask(k0, v >= cut_incl)
            return carry
        lax.fori_loop(0, n_steps, grouped(body), 0)

    def mask_ties():
        def count_body(c, n_gt):
            _, v = chunk(c)
            tcnt_ref[c] = count(v == thr)
            return n_gt + count(v > thr)
        n_gt = lax.fori_loop(0, n_steps, grouped(count_body), jnp.zeros((sub, tq), F32))
        n_gt = jnp.sum(n_gt, axis=0, keepdims=True)
        need = jnp.where(tie, kf - n_gt, float(seq))

        def quota_body(c, carry):
            before, split = carry
            t = jnp.sum(tcnt_ref[c], axis=0, keepdims=True)
            keep = jnp.clip(need - before, 0.0, t)
            tcnt_ref[c, 0:1, :] = keep
            split = jnp.where((keep > 0.0) & (keep < t), c.astype(F32), split)
            return before + t, split
        _, split0 = lax.fori_loop(0, n_steps * group, quota_body,
                                  (jnp.zeros((1, tq), F32), jnp.full((1, tq), -1.0, F32)))

        def split_body(st):
            c_f, split = st
            c = c_f.astype(jnp.int32)
            k0, v = chunk(c)
            is_tie = v == thr
            r = lax.broadcasted_iota(jnp.int32, (ch, ch), 0)
            s = lax.broadcasted_iota(jnp.int32, (ch, ch), 1)
            lower = jnp.where(s < r, 1.0, 0.0).astype(BF16)
            rank = jnp.dot(lower, jnp.where(is_tie, 1.0, 0.0).astype(BF16), preferred_element_type=F32)
            o_ref[0, pl.ds(k0, ch), :] = jnp.where(is_tie & (rank >= tcnt_ref[c, 0:1, :]), -jnp.inf, v)
            split = jnp.where(split == c_f, -1.0, split)
            return jnp.max(split), split
        lax.while_loop(lambda st: st[0] >= 0.0, split_body, (jnp.max(split0), split0))

        def body(c, carry):
            k0, v = chunk(c)
            keep_ties = tcnt_ref[c, 0:1, :] > 0.0
            write_mask(k0, v >= jnp.where(keep_ties, cut_incl, cut_excl))
            return carry
        lax.fori_loop(0, n_steps, grouped(body), 0)

    lax.cond(n_tie > 0.0, mask_ties, mask_plain)

    def fill_chunk(c, carry):
        k0 = pl.multiple_of(c * ch, ch)
        o_ref[0, pl.ds(k0, ch), :] = jnp.full((ch, tq), NEG, F32)
        return carry

    lax.fori_loop(n_chunks, seq // ch, fill_chunk, 0)


def _indexer(iqT, iwT, ik, topk):
    bsz, _, seq = iqT.shape
    tq = min(TQ, seq)
    return pl.pallas_call(
        functools.partial(_indexer_kernel, topk=topk, seq=seq),
        grid=(bsz, seq // tq),
        in_specs=[
            pl.BlockSpec((1, IDX_HEADS * IDX_DIM, tq), lambda b, i: (b, 0, i)),
            pl.BlockSpec((1, IDX_HEADS, tq), lambda b, i: (b, 0, i)),
            pl.BlockSpec((1, seq, IDX_DIM), lambda b, i: (b, 0, 0)),
        ],
        out_specs=pl.BlockSpec((1, seq, tq), lambda b, i: (b, 0, i)),
        out_shape=jax.ShapeDtypeStruct((bsz, seq, seq), F32),
        scratch_shapes=[pltpu.VMEM((max(seq // IDX_CHUNK, 1), 8, tq), F32)],
        compiler_params=pltpu.CompilerParams(
            dimension_semantics=("parallel", "parallel"), vmem_limit_bytes=VMEM_LIMIT),
        name="dsa_indexer",
    )(iqT, iwT, ik)


def _attn_kernel(*refs, n_heads, dk, dv, use_mask):
    if use_mask:
        (pq_ref, pk_ref, qmin_ref, kmax_ref, reg_ref, qT_ref, k_ref, vT_ref, mask_ref, qpos_ref, kpos_ref,
         btab_ref, band_ref, o_ref, m_sc, acc_sc, s_buf, p_buf) = refs
    else:
        pq_ref, pk_ref, qT_ref, k_ref, vT_ref, o_ref, m_sc, acc_sc, s_buf, p_buf = refs
    tq = qT_ref.shape[2]
    tk = k_ref.shape[1]
    b = pl.program_id(0)
    pair = pl.program_id(1)
    qi = pq_ref[pair]
    ki = pk_ref[pair]
    last_k = ((qi + 1) * tq - 1) // tk

    @pl.when(ki == 0)
    def _init():
        m_sc[...] = jnp.full(m_sc.shape, NEG, F32)
        acc_sc[...] = jnp.zeros(acc_sc.shape, F32)

    ts = s_buf.shape[1]
    ck = min(ATTN_CHUNK, ts)
    n_chunks = ts // ck
    dva = vT_ref.shape[1] // n_heads
    rows = lambda g: slice(g * ck, (g + 1) * ck)

    def step(extra):
        n_buf = s_buf.shape[0]
        n_pbuf = p_buf.shape[0]
        ahead = n_buf - 1
        units = [(j, h) for j in range(tk // ts) for h in range(n_heads)]

        def logits_to_buf(i):
            j, h = units[i]
            part = None
            for c in range(n_chunks):
                g = j * n_chunks + c
                s = jnp.dot(k_ref[0, rows(g), h * dk:(h + 1) * dk], qT_ref[0, h * dk:(h + 1) * dk, :],
                            preferred_element_type=F32)
                s = extra(h, g, s)
                s_buf[i % n_buf, c * ck:(c + 1) * ck, :] = s
                part = s if part is None else jnp.maximum(part, s)
            return jnp.max(part, axis=0, keepdims=True)

        tile_max = {i: logits_to_buf(i) for i in range(min(ahead, len(units)))}
        for i, (j, h) in enumerate(units):
            m_prev = m_sc[h:h + 1, :]
            m_new = jnp.maximum(m_prev, tile_max.pop(i))
            alpha = jnp.exp2(m_prev - m_new)
            if i + ahead < len(units):
                tile_max[i + ahead] = logits_to_buf(i + ahead)
            p_buf[i % n_pbuf] = jnp.exp2(s_buf[i % n_buf] - m_new).astype(BF16)
            pv = jnp.dot(vT_ref[0, h * dva:(h + 1) * dva, j * ts:(j + 1) * ts], p_buf[i % n_pbuf],
                         preferred_element_type=F32)
            acc_sc[h * dva:(h + 1) * dva, :] = acc_sc[h * dva:(h + 1) * dva, :] * alpha + pv
            m_sc[h:h + 1, :] = m_new

    if use_mask:
        near = qmin_ref[b, qi] - kmax_ref[b, ki] < MAX_DISTANCE

        @pl.when(jnp.logical_not(near))
        def _far():
            step(lambda h, c, s: s + mask_ref[0, rows(c), :])

        consecutive = reg_ref[b] > 0

        @pl.when(near & consecutive)
        def _near_banded():
            def extra(h, c, s):
                parts = []
                for j in range(tq // LANES):
                    dist = (qi * tq + j * LANES) - (ki * tk + c * ck)
                    parts.append(jnp.where(dist == 0, band_ref[h, 0],
                                           jnp.where(dist == ck, band_ref[h, 1], 0.0)))
                return s + mask_ref[0, rows(c), :] + jnp.concatenate(parts, axis=1)
            step(extra)

        @pl.when(near & jnp.logical_not(consecutive))
        def _near_general():
            def extra(h, c, s):
                rel = qpos_ref[0] - kpos_ref[0, rows(c), :]
                n = jnp.clip(rel, 0, MAX_DISTANCE - 1)
                tab = jnp.broadcast_to(btab_ref[h:h + 1, :], (ck, LANES))
                parts = [jnp.take_along_axis(tab, n[:, j * LANES:(j + 1) * LANES], axis=1)
                         for j in range(tq // LANES)]
                return s + mask_ref[0, rows(c), :] + jnp.concatenate(parts, axis=1)
            step(extra)
    else:
        diag = (ki + 1) * tk > qi * tq

        @pl.when(jnp.logical_not(diag))
        def _full():
            step(lambda h, c, s: s)

        @pl.when(diag)
        def _diag():
            def extra(h, c, s):
                key_idx = ki * tk + c * ck + lax.broadcasted_iota(jnp.int32, (ck, tq), 0)
                q_idx = qi * tq + lax.broadcasted_iota(jnp.int32, (ck, tq), 1)
                return jnp.where(key_idx <= q_idx, s, NEG)
            step(extra)

    @pl.when(ki == last_k)
    def _fin():
        for h in range(n_heads):
            inv = 1.0 / acc_sc[h * dva + dv:h * dva + dv + 1, :]
            o_ref[0, h * dv:(h + 1) * dv, :] = (acc_sc[h * dva:h * dva + dv, :] * inv).astype(o_ref.dtype)


def _causal_pairs(nq, tq, tk):
    pq, pk = [], []
    for qi in range(nq):
        for ki in range(((qi + 1) * tq - 1) // tk + 1):
            pq.append(qi)
            pk.append(ki)
    return np.asarray(pq, np.int32), np.asarray(pk, np.int32)


def _attention(qT, k, vT, n_heads, dk, dv, mask=None, pos=None, btab=None):
    bsz, _, seq = qT.shape
    dva = dv + ONES_ROWS
    vT = jnp.concatenate([vT.reshape(bsz, n_heads, dv, seq),
                          jnp.ones((bsz, n_heads, ONES_ROWS, seq), vT.dtype)], axis=2)
    vT = vT.reshape(bsz, n_heads * dva, seq)
    tq = min(TQ, seq)
    tk = min(TK, seq)
    ts = min(ATTN_SUB, tk)
    nq, nk = seq // tq, seq // tk
    use_mask = mask is not None
    pq, pk = _causal_pairs(nq, tq, tk)

    in_specs = [
        pl.BlockSpec((1, n_heads * dk, tq), lambda b, s, pq, pk, *_: (b, 0, pq[s])),
        pl.BlockSpec((1, tk, n_heads * dk), lambda b, s, pq, pk, *_: (b, pk[s], 0)),
        pl.BlockSpec((1, n_heads * dva, tk), lambda b, s, pq, pk, *_: (b, 0, pk[s])),
    ]
    args = [qT, k, vT]
    prefetch = [jnp.asarray(pq), jnp.asarray(pk)]
    if use_mask:
        assert min(ATTN_CHUNK, ts) == LANES
        consecutive = jnp.all(pos[:, 1:] - pos[:, :-1] == 1, axis=1).astype(jnp.int32)
        prefetch += [jnp.min(pos.reshape(bsz, nq, tq), axis=-1), jnp.max(pos.reshape(bsz, nk, tk), axis=-1),
                     consecutive]
        off = np.arange(LANES)
        dist = off[None, :] - off[:, None]
        band = jnp.stack([btab[:, np.clip(dist, 0, MAX_DISTANCE - 1)],
                          btab[:, np.clip(dist + LANES, 0, MAX_DISTANCE - 1)]], axis=1)
        in_specs += [
            pl.BlockSpec((1, tk, tq), lambda b, s, pq, pk, *_: (b, pk[s], pq[s])),
            pl.BlockSpec((1, 1, tq), lambda b, s, pq, pk, *_: (b, 0, pq[s])),
            pl.BlockSpec((1, tk, 1), lambda b, s, pq, pk, *_: (b, pk[s], 0)),
            pl.BlockSpec(btab.shape, lambda b, s, pq, pk, *_: (0, 0)),
            pl.BlockSpec(band.shape, lambda b, s, pq, pk, *_: (0, 0, 0, 0)),
        ]
        args += [mask, pos.reshape(bsz, 1, seq), pos.reshape(bsz, seq, 1), btab, band]
    grid_spec = pltpu.PrefetchScalarGridSpec(
        num_scalar_prefetch=len(prefetch),
        grid=(bsz, len(pq)),
        in_specs=in_specs,
        out_specs=pl.BlockSpec((1, n_heads * dv, tq), lambda b, s, pq, pk, *_: (b, 0, pq[s])),
        scratch_shapes=[pltpu.VMEM((n_heads, tq), F32), pltpu.VMEM((n_heads * dva, tq), F32),
                        pltpu.VMEM((ATTN_S_BUFFERS, ts, tq), F32), pltpu.VMEM((ATTN_P_BUFFERS, ts, tq), BF16)],
    )
    return pl.pallas_call(
        functools.partial(_attn_kernel, n_heads=n_heads, dk=dk, dv=dv, use_mask=use_mask),
        grid_spec=grid_spec,
        out_shape=jax.ShapeDtypeStruct((bsz, n_heads * dv, seq), BF16),
        compiler_params=pltpu.CompilerParams(dimension_semantics=("parallel", "arbitrary"),
                                             vmem_limit_bytes=VMEM_LIMIT),
        name="dsa_attention" if use_mask else "mla_attention",
    )(*prefetch, *args)


def _layer_norm(z, g, b):
    mu = jnp.mean(z, axis=-1, keepdims=True)
    zc = z - mu
    var = jnp.mean(zc * zc, axis=-1, keepdims=True)
    return zc * lax.rsqrt(var + LN_EPS) * g + b


def _merge_kernel(x_ref, yaT_ref, ybT_ref, gate_ref, g1_ref, wa_ref, wb_ref, wo_ref, lg_ref, lb_ref,
                  o_ref, *, alpha):
    d = x_ref.shape[2]
    tn = (((0,), (0,)), ((), ()))
    pa = lax.dot_general(yaT_ref[0], wa_ref[...], tn, preferred_element_type=F32)
    pb = lax.dot_general(ybT_ref[0], wb_ref[...], tn, preferred_element_type=F32)
    gate = gate_ref[0].astype(F32)
    merged = gate[:, :d] * pa + gate[:, d:] * pb
    y = jnp.dot(merged.astype(BF16), wo_ref[...], preferred_element_type=F32)
    o_ref[0] = _layer_norm(alpha * x_ref[0] + g1_ref[0] * y, lg_ref[...], lb_ref[...])


def _merge(x, yaT, ybT, gate, g1, wa, wb, wo, lg, lb, alpha):
    bsz, seq, d = x.shape
    tm = min(TM_PROJ, seq)
    tok = lambda width: pl.BlockSpec((1, tm, width), lambda b, i: (b, i, 0))
    tokT = lambda width: pl.BlockSpec((1, width, tm), lambda b, i: (b, 0, i))
    full = lambda a: pl.BlockSpec(a.shape, lambda b, i: (0,) * a.ndim)
    return pl.pallas_call(
        functools.partial(_merge_kernel, alpha=alpha),
        grid=(bsz, seq // tm),
        in_specs=[tok(d), tokT(yaT.shape[1]), tokT(ybT.shape[1]), tok(2 * d),
                  pl.BlockSpec((1, 1, d), lambda b, i: (b, 0, 0)),
                  full(wa), full(wb), full(wo), full(lg), full(lb)],
        out_specs=tok(d),
        out_shape=jax.ShapeDtypeStruct((bsz, seq, d), F32),
        compiler_params=pltpu.CompilerParams(
            dimension_semantics=("parallel", "parallel"), vmem_limit_bytes=VMEM_LIMIT),
        name="merge_out_ln",
    )(x, yaT, ybT, gate, g1, wa, wb, wo, lg, lb)


def _ffn_kernel(x_ref, sc_ref, sh_ref, g2_ref, wup_ref, cw_ref, cb_ref, wdn_ref, lg_ref, lb_ref,
                o_ref, hist_ref, *, alpha, d_ff):
    i = pl.program_id(1)
    tm = x_ref.shape[1]
    x = x_ref[0]
    h = (x * (1.0 + sc_ref[0]) + sh_ref[0]).astype(BF16)

    @pl.when(i == 0)
    def _():
        hist_ref[...] = jnp.zeros(hist_ref.shape, F32)

    fc = math.gcd(d_ff, FFN_CHUNK)
    row = lax.broadcasted_iota(jnp.int32, (tm, fc), 0)

    def up_conv(c0):
        cols = slice(c0, c0 + fc)
        u = jnp.dot(h, wup_ref[:, cols], preferred_element_type=F32)
        prev = hist_ref[:, cols]
        u1 = jnp.where(row == 0, prev[7:8, :], pltpu.roll(u, 1, axis=0))
        u2 = jnp.where(row == 0, prev[6:7, :], jnp.where(row == 1, prev[7:8, :], pltpu.roll(u, 2, axis=0)))
        hist_ref[:, cols] = u[tm - 8:, :]
        return cw_ref[0:1, cols] * u2 + cw_ref[1:2, cols] * u1 + cw_ref[2:3, cols] * u + cb_ref[:, cols]

    y = None
    for j in range(d_ff // fc):
        g = up_conv(j * fc)
        val = up_conv(d_ff + j * fc)
        a = (g * _sigmoid(g) * val).astype(BF16)
        part = jnp.dot(a, wdn_ref[j * fc:(j + 1) * fc, :], preferred_element_type=F32)
        y = part if y is None else y + part
    o_ref[0] = _layer_norm(alpha * x + g2_ref[0] * y, lg_ref[...], lb_ref[...])


def _ffn(x, sc, sh, g2, wup, cw, cb, wdn, lg, lb, alpha):
    bsz, seq, d = x.shape
    d_ff = wdn.shape[0]
    tm = min(TM_FFN, seq)
    tok = pl.BlockSpec((1, tm, d), lambda b, i: (b, i, 0))
    per_b = pl.BlockSpec((1, 1, d), lambda b, i: (b, 0, 0))
    once = lambda a: pl.BlockSpec(a.shape, lambda b, i: (0,) * a.ndim, pipeline_mode=pl.Buffered(1))
    return pl.pallas_call(
        functools.partial(_ffn_kernel, alpha=alpha, d_ff=d_ff),
        grid=(bsz, seq // tm),
        in_specs=[tok, per_b, per_b, per_b, once(wup), once(cw), once(cb), once(wdn), once(lg), once(lb)],
        out_specs=tok,
        out_shape=jax.ShapeDtypeStruct((bsz, seq, d), F32),
        scratch_shapes=[pltpu.VMEM((8, 2 * d_ff), F32)],
        compiler_params=pltpu.CompilerParams(
            dimension_semantics=("parallel", "arbitrary"), vmem_limit_bytes=VMEM_LIMIT),
        name="conv_ffn_ln",
    )(x, sc, sh, g2, wup, cw, cb, wdn, lg, lb)


def kernel(x, c, positions, rel_bias, w_ada, b_ada, w_in, q_norm_g, w_uq, kv_norm_g, w_ukv,
           w_branch_a, w_branch_b, w_out, ln1_g, ln1_b, w_up, conv_w, conv_b, w_down, ln2_g, ln2_b):
    bsz, seq, d = x.shape
    depth = w_ada.shape[0]
    alpha = (2 * depth) ** 0.25
    topk = min(TOPK_MAX, seq // 4)
    assert seq % TQ == 0 or seq < TQ

    mod = _ada_mod(c, w_ada, b_ada)
    pos3 = positions.reshape(bsz, seq, 1)
    invf = ROPE_THETA ** (-jnp.arange(ROPE_HALF, dtype=F32) * (2.0 / ROPE_DIM))
    invf = jnp.tile(invf, LANES // ROPE_HALF).reshape(1, LANES)
    btab = ((rel_bias[_BUCKET_TABLE] - rel_bias[NUM_BUCKETS - 1][None, :]) * LOG2E).T.astype(F32)

    for l in range(depth):
        sh1, sc1, g1, sh2, sc2, g2 = [mod[l, :, None, j * d:(j + 1) * d] for j in range(6)]
        qkva, iq, ik, iw, qb, kvb, krb, gate = _inproj(
            x, sc1, sh1, pos3, invf, _pack_w_in(w_in[l], d),
            q_norm_g[l].reshape(1, -1), _pack_w_uq(w_uq[l]),
            kv_norm_g[l].reshape(1, -1), _pack_w_ukv(w_ukv[l]))

        mask = _indexer(iq.transpose(0, 2, 1), iw.transpose(0, 2, 1), ik, topk)
        qaT = qkva[:, :, :A_WIDTH].transpose(0, 2, 1)
        ka = qkva[:, :, A_WIDTH:2 * A_WIDTH]
        vaT = qkva[:, :, 2 * A_WIDTH:].transpose(0, 2, 1)
        yaT = _attention(qaT, ka, vaT, A_HEADS, A_HEAD_DIM, A_HEAD_DIM, mask=mask, pos=positions, btab=btab)

        n_nope = B_HEADS * NOPE_DIM
        n_r = B_HEADS * ROPE_HALF
        q_heads = jnp.concatenate([
            qb[:, :, :n_nope].reshape(bsz, seq, B_HEADS, NOPE_DIM),
            qb[:, :, n_nope:n_nope + n_r].reshape(bsz, seq, B_HEADS, ROPE_HALF),
            qb[:, :, n_nope + n_r:].reshape(bsz, seq, B_HEADS, ROPE_HALF)], axis=-1)
        qbT = q_heads.reshape(bsz, seq, B_HEADS * B_QK_DIM).transpose(0, 2, 1)
        kr = jnp.concatenate([krb[:, :, :ROPE_HALF], krb[:, :, LANES:LANES + ROPE_HALF]], axis=-1)
        k_heads = jnp.concatenate([
            kvb[:, :, :n_nope].reshape(bsz, seq, B_HEADS, NOPE_DIM),
            jnp.broadcast_to(kr[:, :, None, :], (bsz, seq, B_HEADS, ROPE_DIM))], axis=-1)
        kb = k_heads.reshape(bsz, seq, B_HEADS * B_QK_DIM)
        vbT = kvb[:, :, n_nope:].transpose(0, 2, 1)
        ybT = _attention(qbT, kb, vbT, B_HEADS, B_QK_DIM, V_DIM)

        x = _merge(x, yaT, ybT, gate, g1,
                   w_branch_a[l].astype(BF16), w_branch_b[l].astype(BF16), w_out[l].astype(BF16),
                   ln1_g[l].reshape(1, d), ln1_b[l].reshape(1, d), alpha)
        x = _ffn(x, sc2, sh2, g2, w_up[l].astype(BF16), conv_w[l], conv_b[l].reshape(1, -1),
                 w_down[l].astype(BF16), ln2_g[l].reshape(1, d), ln2_b[l].reshape(1, d), alpha)
    return x
```

```python
import functools
import math

import numpy as np
import jax
import jax.numpy as jnp
from jax import lax
from jax.experimental import pallas as pl
from jax.experimental.pallas import tpu as pltpu

F32 = jnp.float32
BF16 = jnp.bfloat16

A_HEADS = 8
A_HEAD_DIM = 64
IDX_HEADS = 8
IDX_DIM = 64
TOPK_MAX = 256
B_HEADS = 8
Q_RANK = 256
KV_RANK = 128
NOPE_DIM = 64
ROPE_DIM = 32
ROPE_HALF = ROPE_DIM // 2
V_DIM = 64
ROPE_THETA = 10000.0
NUM_BUCKETS = 32
MAX_DISTANCE = 128
CONV_WIDTH = 3
LN_EPS = 1e-5
RMS_EPS = 1e-6
A_WIDTH = A_HEADS * A_HEAD_DIM
B_WIDTH = B_HEADS * V_DIM
B_QK_DIM = NOPE_DIM + ROPE_DIM

LANES = 128
V7X_VMEM_BYTES = 64 * 1024 * 1024
VMEM_LIMIT = 52 * 1024 * 1024

NEG = -1e30
LOG2E = math.log2(math.e)

TM_PROJ = 512
TM_FFN = 512
FFN_CHUNK = 1408
TQ = 256
TK = 1024
IDX_CHUNK = 256
IDX_SCORE_CHUNK = 256
IDX_SCORE_GROUP = 4
IDX_COUNT_GROUP = 4
ATTN_SUB = 512
ATTN_CHUNK = 128
ONES_ROWS = 16
ATTN_P_BUFFERS = 2
ATTN_S_BUFFERS = 4


def _t5_bucket_table():
    n = np.arange(MAX_DISTANCE)
    max_exact = NUM_BUCKETS // 2
    out = []
    for dt in (np.float32, np.float64):
        ratio = np.log(np.maximum(n, 1).astype(dt) / dt(max_exact)) / dt(math.log(MAX_DISTANCE / max_exact))
        large = max_exact + (ratio * dt(NUM_BUCKETS - max_exact)).astype(np.int32)
        large = np.minimum(large, NUM_BUCKETS - 1)
        out.append(np.where(n < max_exact, n, large))
    assert (out[0] == out[1]).all()
    assert out[0][-1] == NUM_BUCKETS - 1
    return out[0].astype(np.int32)


_BUCKET_TABLE = _t5_bucket_table()


def _ada_kernel(c_ref, w_ref, b_ref, o_ref):
    c = c_ref[...]
    s = c * (1.0 / (1.0 + jnp.exp(-c)))
    o_ref[0] = jnp.dot(s, w_ref[0], preferred_element_type=F32) + b_ref[0]


def _ada_mod(c, w_ada, b_ada):
    depth, d, d6 = w_ada.shape
    bsz = c.shape[0]
    n_chunks = d6 // d
    return pl.pallas_call(
        _ada_kernel,
        grid=(depth, n_chunks),
        in_specs=[
            pl.BlockSpec((bsz, d), lambda l, j: (0, 0)),
            pl.BlockSpec((1, d, d), lambda l, j: (l, 0, j)),
            pl.BlockSpec((1, 1, d), lambda l, j: (l, 0, j)),
        ],
        out_specs=pl.BlockSpec((1, bsz, d), lambda l, j: (l, 0, j)),
        out_shape=jax.ShapeDtypeStruct((depth, bsz, d6), F32),
        name="ada_mod",
    )(c, w_ada, b_ada.reshape(depth, 1, d6))


_C_QKV = 0
_C_IQ = _C_QKV + 3 * A_WIDTH
_C_IK = _C_IQ + IDX_HEADS * IDX_DIM
_C_IW = _C_IK + LANES
_C_CQ = _C_IW + LANES
_C_CKV = _C_CQ + Q_RANK
_C_KR1 = _C_CKV + KV_RANK
_C_KR2 = _C_KR1 + LANES
_C_GATE = _C_KR2 + LANES


def _pack_w_in(w_in, d):
    sizes = (A_WIDTH, A_WIDTH, A_WIDTH, IDX_HEADS * IDX_DIM, IDX_DIM, IDX_HEADS,
             Q_RANK, KV_RANK, ROPE_DIM, d, d)
    offs = np.cumsum((0,) + sizes)
    seg = {name: w_in[:, offs[i]:offs[i + 1]] for i, name in enumerate(
        ("qa", "ka", "va", "iq", "ik", "iw", "cq", "ckv", "kr", "ga", "gb"))}

    def pad(w, width):
        return jnp.pad(w, ((0, 0), (0, width - w.shape[1])))

    parts = [seg["qa"], seg["ka"], seg["va"], seg["iq"],
             pad(seg["ik"], LANES), pad(seg["iw"], LANES),
             seg["cq"], seg["ckv"],
             pad(seg["kr"][:, :ROPE_HALF], LANES), pad(seg["kr"][:, ROPE_HALF:], LANES),
             seg["ga"], seg["gb"]]
    return jnp.concatenate(parts, axis=1).astype(BF16)


def _pack_w_uq(w_uq):
    w = w_uq.reshape(Q_RANK, B_HEADS, B_QK_DIM)
    nope = w[:, :, :NOPE_DIM].reshape(Q_RANK, B_HEADS * NOPE_DIM)
    r1 = w[:, :, NOPE_DIM:NOPE_DIM + ROPE_HALF].reshape(Q_RANK, B_HEADS * ROPE_HALF)
    r2 = w[:, :, NOPE_DIM + ROPE_HALF:].reshape(Q_RANK, B_HEADS * ROPE_HALF)
    return jnp.concatenate([nope, r1, r2], axis=1).astype(BF16)


def _pack_w_ukv(w_ukv):
    w = w_ukv.reshape(KV_RANK, B_HEADS, NOPE_DIM + V_DIM)
    kn = w[:, :, :NOPE_DIM].reshape(KV_RANK, B_HEADS * NOPE_DIM)
    v = w[:, :, NOPE_DIM:].reshape(KV_RANK, B_HEADS * V_DIM)
    return jnp.concatenate([kn, v], axis=1).astype(BF16)


def _sigmoid(x):
    return 1.0 / (1.0 + jnp.exp(-x))


def _rms(x, g):
    return x * lax.rsqrt(jnp.mean(x * x, axis=-1, keepdims=True) + RMS_EPS) * g


def _inproj_kernel(x_ref, sc_ref, sh_ref, pos_ref, invf_ref, w_ref, qg_ref, wuq_ref, kvg_ref, wukv_ref,
                   o_qkva, o_iq, o_ik, o_iw, o_qb, o_kvb, o_krb, o_gate, *, d_model):
    h = (x_ref[0] * (1.0 + sc_ref[0]) + sh_ref[0]).astype(BF16)

    def proj(c0, width):
        return jnp.dot(h, w_ref[:, c0:c0 + width], preferred_element_type=F32)

    a_scale = A_HEAD_DIM ** -0.5 * LOG2E
    o_qkva[0, :, 0:A_WIDTH] = (proj(_C_QKV, A_WIDTH) * a_scale).astype(BF16)
    o_qkva[0, :, A_WIDTH:3 * A_WIDTH] = proj(_C_QKV + A_WIDTH, 2 * A_WIDTH).astype(BF16)
    o_iq[0] = proj(_C_IQ, IDX_HEADS * IDX_DIM).astype(BF16)
    o_ik[0] = proj(_C_IK, LANES)[:, :IDX_DIM].astype(BF16)
    o_iw[0] = proj(_C_IW, LANES)[:, :IDX_HEADS] * (IDX_DIM ** -0.5 * IDX_HEADS ** -0.5)

    ang = pos_ref[0].astype(F32) * invf_ref[...]
    cos, sin = jnp.cos(ang), jnp.sin(ang)

    b_scale = B_QK_DIM ** -0.5 * LOG2E
    cq = _rms(proj(_C_CQ, Q_RANK), qg_ref[...]).astype(BF16)
    q = jnp.dot(cq, wuq_ref[...], preferred_element_type=F32)
    n_nope = B_HEADS * NOPE_DIM
    n_r = B_HEADS * ROPE_HALF
    x1 = q[:, n_nope:n_nope + n_r]
    x2 = q[:, n_nope + n_r:]
    o_qb[0, :, 0:n_nope] = (q[:, :n_nope] * b_scale).astype(BF16)
    o_qb[0, :, n_nope:n_nope + n_r] = ((x1 * cos - x2 * sin) * b_scale).astype(BF16)
    o_qb[0, :, n_nope + n_r:] = ((x1 * sin + x2 * cos) * b_scale).astype(BF16)

    ckv = _rms(proj(_C_CKV, KV_RANK), kvg_ref[...]).astype(BF16)
    o_kvb[0] = jnp.dot(ckv, wukv_ref[...], preferred_element_type=F32).astype(BF16)

    k1 = proj(_C_KR1, LANES)
    k2 = proj(_C_KR2, LANES)
    o_krb[0, :, 0:LANES] = (k1 * cos - k2 * sin).astype(BF16)
    o_krb[0, :, LANES:] = (k1 * sin + k2 * cos).astype(BF16)

    o_gate[0] = _sigmoid(proj(_C_GATE, 2 * d_model)).astype(BF16)


def _inproj(x, sc, sh, pos3, invf, w_pack, qg, wuq, kvg, wukv):
    bsz, seq, d = x.shape
    tm = min(TM_PROJ, seq)
    grid = (bsz, seq // tm)
    tok = lambda width: pl.BlockSpec((1, tm, width), lambda b, i: (b, i, 0))
    per_b = pl.BlockSpec((1, 1, d), lambda b, i: (b, 0, 0))
    full = lambda a: pl.BlockSpec(a.shape, lambda b, i: (0,) * a.ndim)
    out_widths = (3 * A_WIDTH, IDX_HEADS * IDX_DIM, IDX_DIM, IDX_HEADS,
                  B_HEADS * B_QK_DIM, B_HEADS * (NOPE_DIM + V_DIM), 2 * LANES, 2 * d)
    out_dtypes = (BF16, BF16, BF16, F32, BF16, BF16, BF16, BF16)
    return pl.pallas_call(
        functools.partial(_inproj_kernel, d_model=d),
        grid=grid,
        in_specs=[tok(d), per_b, per_b, tok(1), full(invf), full(w_pack), full(qg), full(wuq),
                  full(kvg), full(wukv)],
        out_specs=[tok(w) for w in out_widths],
        out_shape=[jax.ShapeDtypeStruct((bsz, seq, w), dt) for w, dt in zip(out_widths, out_dtypes)],
        compiler_params=pltpu.CompilerParams(
            dimension_semantics=("parallel", "parallel"), vmem_limit_bytes=VMEM_LIMIT),
        name="in_proj",
    )(x, sc, sh, pos3, invf, w_pack, qg, wuq, kvg, wukv)


def _ordered_key(x):
    b = lax.bitcast_convert_type(x, jnp.int32)
    return jnp.where(b < 0, b ^ jnp.int32(0x7FFFFFFF), b)


def _from_ordered_key(k):
    b = jnp.where(k < 0, k ^ jnp.int32(0x7FFFFFFF), k)
    return lax.bitcast_convert_type(b, F32)


def _indexer_kernel(iqT_ref, iwT_ref, ik_ref, o_ref, tcnt_ref, *, topk, seq):
    tq = o_ref.shape[2]
    ch = IDX_CHUNK
    sch = min(IDX_SCORE_CHUNK, ch)
    qi = pl.program_id(1)
    n_chunks = (qi * tq) // ch + tq // ch
    sub = 8
    kf = float(topk)

    def fold(v, op):
        return op(v.reshape(v.shape[0] // sub, sub, tq), axis=0)

    def count(pred):
        return fold(jnp.where(pred, 1.0, 0.0), jnp.sum)

    def chunk(c):
        k0 = pl.multiple_of(c * ch, ch)
        return k0, o_ref[0, pl.ds(k0, ch), :]

    def score_chunk(c, carry, diagonal):
        vmax, vmin, n_pos, n_nonneg = carry
        k0 = pl.multiple_of(c * sch, sch)
        ik = ik_ref[0, pl.ds(k0, sch), :]
        acc = jnp.zeros((sch, tq), F32)
        for hd in range(IDX_HEADS):
            y = jnp.dot(ik, iqT_ref[0, hd * IDX_DIM:(hd + 1) * IDX_DIM, :],
                        preferred_element_type=F32)
            acc = acc + jnp.maximum(y, 0.0) * iwT_ref[0, hd:hd + 1, :]
        if diagonal:
            key_idx = k0 + lax.broadcasted_iota(jnp.int32, (sch, tq), 0)
            q_idx = qi * tq + lax.broadcasted_iota(jnp.int32, (sch, tq), 1)
            causal = key_idx <= q_idx
            acc = jnp.where(causal, acc, -jnp.inf)
            vmin = jnp.minimum(vmin, fold(jnp.where(causal, acc, jnp.inf), jnp.min))
        else:
            vmin = jnp.minimum(vmin, fold(acc, jnp.min))
        o_ref[0, pl.ds(k0, sch), :] = acc
        vmax = jnp.maximum(vmax, fold(acc, jnp.max))
        n_pos = n_pos + count(acc > 0.0)
        n_nonneg = n_nonneg + count(acc >= 0.0)
        return vmax, vmin, n_pos, n_nonneg

    n_score = n_chunks * (ch // sch)
    n_diag = tq // sch
    n_full = n_score - n_diag
    sg = IDX_SCORE_GROUP

    def score_group(g, carry):
        for i in range(sg):
            carry = score_chunk(g * sg + i, carry, diagonal=False)
        return carry

    stats = (jnp.full((sub, tq), -jnp.inf, F32), jnp.full((sub, tq), jnp.inf, F32),
             jnp.zeros((sub, tq), F32), jnp.zeros((sub, tq), F32))
    stats = lax.fori_loop(0, n_full // sg, score_group, stats)
    stats = lax.fori_loop((n_full // sg) * sg, n_full, functools.partial(score_chunk, diagonal=False), stats)
    stats = lax.fori_loop(n_full, n_score, functools.partial(score_chunk, diagonal=True), stats)
    row_max = jnp.max(stats[0], axis=0, keepdims=True)
    row_min = jnp.min(stats[1], axis=0, keepdims=True)
    n_pos = jnp.sum(stats[2], axis=0, keepdims=True)
    n_nonneg = jnp.sum(stats[3], axis=0, keepdims=True)

    group = math.gcd(seq // ch, IDX_COUNT_GROUP)
    n_steps = (n_chunks + group - 1) // group

    def pad_chunk(c, carry):
        o_ref[0, pl.ds(pl.multiple_of(c * ch, ch), ch), :] = jnp.full((ch, tq), -jnp.inf, F32)
        return carry
    lax.fori_loop(n_chunks, n_steps * group, pad_chunk, 0)

    def count_ge(thr):
        def body(c, parts):
            return tuple(p + count(chunk(c * group + i)[1] >= thr) for i, p in enumerate(parts))
        parts = lax.fori_loop(0, n_steps, body, (jnp.zeros((sub, tq), F32),) * group)
        return jnp.sum(functools.reduce(jnp.add, parts), axis=0, keepdims=True)

    n_valid = (qi * tq + lax.broadcasted_iota(jnp.int32, (1, tq), 1) + 1).astype(F32)

    keep_all = n_valid <= kf
    zero_tie = jnp.logical_not(keep_all) & (n_pos < kf) & (n_nonneg >= kf)
    positive = n_pos >= kf
    lo_key0 = jnp.where(positive, 1, _ordered_key(row_min))
    hi_key0 = jnp.where(positive, _ordered_key(row_max) + 1, -2)
    done0 = jnp.where(keep_all | zero_tie, 1, 0).astype(jnp.int32)
    thr0 = jnp.where(keep_all, -jnp.inf, jnp.where(zero_tie, 0.0, row_min))
    tie0 = jnp.where(zero_tie, 1, 0).astype(jnp.int32)

    def n_active(done):
        return jnp.sum(jnp.where(done > 0, 0.0, 1.0))

    def cond(st):
        return st[0] > 0.0

    def body(st):
        _, it, lo_key, hi_key, thr, done, tie = st
        lo = _from_ordered_key(lo_key)
        hi = _from_ordered_key(hi_key)
        mid_val_key = _ordered_key(lo * 0.5 + hi * 0.5)
        mid_bit_key = (lo_key & hi_key) + ((lo_key ^ hi_key) >> 1)
        use_val = (it < 4) & (jnp.abs(hi) < jnp.inf) & (jnp.abs(lo) < jnp.inf)
        mid_key = jnp.where(use_val, mid_val_key, mid_bit_key)
        mid_key = jnp.minimum(jnp.maximum(mid_key, lo_key + 1), hi_key - 1)
        mid = _from_ordered_key(mid_key)
        cnt = count_ge(mid)
        active = done == 0
        ge = cnt >= kf
        lo_key = jnp.where(active & ge, mid_key, lo_key)
        hi_key = jnp.where(active & jnp.logical_not(ge), mid_key, hi_key)
        hit = active & (cnt == kf)
        stuck = active & jnp.logical_not(hit) & (hi_key - 1 <= lo_key)
        thr = jnp.where(hit, mid, jnp.where(stuck, _from_ordered_key(lo_key), thr))
        tie = jnp.where(stuck, 1, tie)
        done = jnp.where(hit | stuck, 1, done)
        return n_active(done), it + 1, lo_key, hi_key, thr, done, tie

    st = lax.while_loop(cond, body, (n_active(done0), jnp.int32(0), lo_key0, hi_key0, thr0, done0, tie0))
    thr, tie = st[4], st[6] > 0
    n_tie = jnp.sum(jnp.where(tie, 1.0, 0.0))

    def write_mask(k0, sel):
        o_ref[0, pl.ds(k0, ch), :] = jnp.where(sel, 0.0, NEG)

    def grouped(per_chunk):
        def body(g, carry):
            for i in range(group):
                carry = per_chunk(g * group + i, carry)
            return carry
        return body

    f32_info = jnp.finfo(F32)
    cut_incl = jnp.where(thr == -jnp.inf, float(f32_info.min), thr)
    above = _from_ordered_key(_ordered_key(thr) + 1)
    cut_excl = jnp.where((above >= 0.0) & (above < float(f32_info.tiny)), float(f32_info.tiny), above)

    def mask_plain():
        def body(c, carry):
            k0, v = chunk(c)
            write_mask(k0, v >= cut_incl)
            return carry
        lax.fori_loop(0, n_steps, grouped(body), 0)

    def mask_ties():
        def count_body(c, n_gt):
            _, v = chunk(c)
            tcnt_ref[c] = count(v == thr)
            return n_gt + count(v > thr)
        n_gt = lax.fori_loop(0, n_steps, grouped(count_body), jnp.zeros((sub, tq), F32))
        n_gt = jnp.sum(n_gt, axis=0, keepdims=True)
        need = jnp.where(tie, kf - n_gt, float(seq))

        def quota_body(c, carry):
            before, split = carry
            t = jnp.sum(tcnt_ref[c], axis=0, keepdims=True)
            keep = jnp.clip(need - before, 0.0, t)
            tcnt_ref[c, 0:1, :] = keep
            split = jnp.where((keep > 0.0) & (keep < t), c.astype(F32), split)
            return before + t, split
        _, split0 = lax.fori_loop(0, n_steps * group, quota_body,
                                  (jnp.zeros((1, tq), F32), jnp.full((1, tq), -1.0, F32)))

        def split_body(st):
            c_f, split = st
            c = c_f.astype(jnp.int32)
            k0, v = chunk(c)
            is_tie = v == thr
            r = lax.broadcasted_iota(jnp.int32, (ch, ch), 0)
            s = lax.broadcasted_iota(jnp.int32, (ch, ch), 1)
            lower = jnp.where(s < r, 1.0, 0.0).astype(BF16)
            rank = jnp.dot(lower, jnp.where(is_tie, 1.0, 0.0).astype(BF16), preferred_element_type=F32)
            o_ref[0, pl.ds(k0, ch), :] = jnp.where(is_tie & (rank >= tcnt_ref[c, 0:1, :]), -jnp.inf, v)
            split = jnp.where(split == c_f, -1.0, split)
            return jnp.max(split), split
        lax.while_loop(lambda st: st[0] >= 0.0, split_body, (jnp.max(split0), split0))

        def body(c, carry):
            k0, v = chunk(c)
            keep_ties = tcnt_ref[c, 0:1, :] > 0.0
            write_mask(k0, v >= jnp.where(keep_ties, cut_incl, cut_excl))
            return carry
        lax.fori_loop(0, n_steps, grouped(body), 0)

    lax.cond(n_tie > 0.0, mask_ties, mask_plain)

    def fill_chunk(c, carry):
        k0 = pl.multiple_of(c * ch, ch)
        o_ref[0, pl.ds(k0, ch), :] = jnp.full((ch, tq), NEG, F32)
        return carry

    lax.fori_loop(n_chunks, seq // ch, fill_chunk, 0)


def _indexer(iqT, iwT, ik, topk):
    bsz, _, seq = iqT.shape
    tq = min(TQ, seq)
    return pl.pallas_call(
        functools.partial(_indexer_kernel, topk=topk, seq=seq),
        grid=(bsz, seq // tq),
        in_specs=[
            pl.BlockSpec((1, IDX_HEADS * IDX_DIM, tq), lambda b, i: (b, 0, i)),
            pl.BlockSpec((1, IDX_HEADS, tq), lambda b, i: (b, 0, i)),
            pl.BlockSpec((1, seq, IDX_DIM), lambda b, i: (b, 0, 0)),
        ],
        out_specs=pl.BlockSpec((1, seq, tq), lambda b, i: (b, 0, i)),
        out_shape=jax.ShapeDtypeStruct((bsz, seq, seq), F32),
        scratch_shapes=[pltpu.VMEM((max(seq // IDX_CHUNK, 1), 8, tq), F32)],
        compiler_params=pltpu.CompilerParams(
            dimension_semantics=("parallel", "parallel"), vmem_limit_bytes=VMEM_LIMIT),
        name="dsa_indexer",
    )(iqT, iwT, ik)


def _attn_kernel(*refs, n_heads, dk, dv, use_mask):
    if use_mask:
        (pq_ref, pk_ref, qmin_ref, kmax_ref, reg_ref, qT_ref, k_ref, vT_ref, mask_ref, qpos_ref, kpos_ref,
         btab_ref, band_ref, o_ref, m_sc, acc_sc, s_buf, p_buf) = refs
    else:
        pq_ref, pk_ref, qT_ref, k_ref, vT_ref, o_ref, m_sc, acc_sc, s_buf, p_buf = refs
    tq = qT_ref.shape[2]
    tk = k_ref.shape[1]
    b = pl.program_id(0)
    pair = pl.program_id(1)
    qi = pq_ref[pair]
    ki = pk_ref[pair]
    last_k = ((qi + 1) * tq - 1) // tk

    @pl.when(ki == 0)
    def _init():
        m_sc[...] = jnp.full(m_sc.shape, NEG, F32)
        acc_sc[...] = jnp.zeros(acc_sc.shape, F32)

    ts = s_buf.shape[1]
    ck = min(ATTN_CHUNK, ts)
    n_chunks = ts // ck
    dva = vT_ref.shape[1] // n_heads
    rows = lambda g: slice(g * ck, (g + 1) * ck)

    n_sub = tk // ts
    n_live = jnp.minimum(((qi + 1) * tq - ki * tk + ts - 1) // ts, n_sub)

    def step(extra, live=n_sub):
        n_buf = s_buf.shape[0]
        n_pbuf = p_buf.shape[0]
        ahead = n_buf - 1
        units = [(j, h) for j in range(live) for h in range(n_heads)]

        def logits_to_buf(i):
            j, h = units[i]
            part = None
            for c in range(n_chunks):
                g = j * n_chunks + c
                s = jnp.dot(k_ref[0, rows(g), h * dk:(h + 1) * dk], qT_ref[0, h * dk:(h + 1) * dk, :],
                            preferred_element_type=F32)
                s = extra(h, g, s)
                s_buf[i % n_buf, c * ck:(c + 1) * ck, :] = s
                part = s if part is None else jnp.maximum(part, s)
            return jnp.max(part, axis=0, keepdims=True)

        tile_max = {i: logits_to_buf(i) for i in range(min(ahead, len(units)))}
        for i, (j, h) in enumerate(units):
            m_prev = m_sc[h:h + 1, :]
            m_new = jnp.maximum(m_prev, tile_max.pop(i))
            alpha = jnp.exp2(m_prev - m_new)
            if i + ahead < len(units):
                tile_max[i + ahead] = logits_to_buf(i + ahead)
            p_buf[i % n_pbuf] = jnp.exp2(s_buf[i % n_buf] - m_new).astype(BF16)
            pv = jnp.dot(vT_ref[0, h * dva:(h + 1) * dva, j * ts:(j + 1) * ts], p_buf[i % n_pbuf],
                         preferred_element_type=F32)
            acc_sc[h * dva:(h + 1) * dva, :] = acc_sc[h * dva:(h + 1) * dva, :] * alpha + pv
            m_sc[h:h + 1, :] = m_new

    if use_mask:
        near = qmin_ref[b, qi] - kmax_ref[b, ki] < MAX_DISTANCE

        @pl.when(jnp.logical_not(near))
        def _far():
            step(lambda h, c, s: s + mask_ref[0, rows(c), :])

        consecutive = reg_ref[b] > 0

        def banded(h, c, s):
            parts = []
            for j in range(tq // LANES):
                dist = (qi * tq + j * LANES) - (ki * tk + c * ck)
                parts.append(jnp.where(dist == 0, band_ref[h, 0],
                                       jnp.where(dist == ck, band_ref[h, 1], 0.0)))
            return s + mask_ref[0, rows(c), :] + jnp.concatenate(parts, axis=1)

        for live in range(1, n_sub + 1):
            pl.when(near & consecutive & (n_live == live))(functools.partial(step, banded, live))

        @pl.when(near & jnp.logical_not(consecutive))
        def _near_general():
            def extra(h, c, s):
                rel = qpos_ref[0] - kpos_ref[0, rows(c), :]
                n = jnp.clip(rel, 0, MAX_DISTANCE - 1)
                tab = jnp.broadcast_to(btab_ref[h:h + 1, :], (ck, LANES))
                parts = [jnp.take_along_axis(tab, n[:, j * LANES:(j + 1) * LANES], axis=1)
                         for j in range(tq // LANES)]
                return s + mask_ref[0, rows(c), :] + jnp.concatenate(parts, axis=1)
            step(extra)
    else:
        diag = (ki + 1) * tk > qi * tq

        @pl.when(jnp.logical_not(diag))
        def _full():
            step(lambda h, c, s: s)

        def causal(h, c, s):
            key_idx = ki * tk + c * ck + lax.broadcasted_iota(jnp.int32, (ck, tq), 0)
            q_idx = qi * tq + lax.broadcasted_iota(jnp.int32, (ck, tq), 1)
            return jnp.where(key_idx <= q_idx, s, NEG)

        for live in range(1, n_sub + 1):
            pl.when(diag & (n_live == live))(functools.partial(step, causal, live))

    @pl.when(ki == last_k)
    def _fin():
        for h in range(n_heads):
            inv = 1.0 / acc_sc[h * dva + dv:h * dva + dv + 1, :]
            o_ref[0, h * dv:(h + 1) * dv, :] = (acc_sc[h * dva:h * dva + dv, :] * inv).astype(o_ref.dtype)


def _causal_pairs(nq, tq, tk):
    pq, pk = [], []
    for qi in range(nq):
        for ki in range(((qi + 1) * tq - 1) // tk + 1):
            pq.append(qi)
            pk.append(ki)
    return np.asarray(pq, np.int32), np.asarray(pk, np.int32)


def _bias_band(btab):
    off = np.arange(LANES)
    dist = off[None, :] - off[:, None]
    idx = np.stack([np.clip(dist, 0, MAX_DISTANCE - 1), np.clip(dist + LANES, 0, MAX_DISTANCE - 1)])
    return btab[:, idx]


def _attention(qT, k, vT, n_heads, dk, dv, mask=None, pos=None, btab=None, band=None):
    bsz, _, seq = qT.shape
    dva = dv + ONES_ROWS
    vT = jnp.concatenate([vT.reshape(bsz, n_heads, dv, seq),
                          jnp.ones((bsz, n_heads, ONES_ROWS, seq), vT.dtype)], axis=2)
    vT = vT.reshape(bsz, n_heads * dva, seq)
    tq = min(TQ, seq)
    tk = min(TK, seq)
    ts = min(ATTN_SUB, tk)
    nq, nk = seq // tq, seq // tk
    use_mask = mask is not None
    pq, pk = _causal_pairs(nq, tq, tk)

    in_specs = [
        pl.BlockSpec((1, n_heads * dk, tq), lambda b, s, pq, pk, *_: (b, 0, pq[s])),
        pl.BlockSpec((1, tk, n_heads * dk), lambda b, s, pq, pk, *_: (b, pk[s], 0)),
        pl.BlockSpec((1, n_heads * dva, tk), lambda b, s, pq, pk, *_: (b, 0, pk[s])),
    ]
    args = [qT, k, vT]
    prefetch = [jnp.asarray(pq), jnp.asarray(pk)]
    if use_mask:
        assert min(ATTN_CHUNK, ts) == LANES
        consecutive = jnp.all(pos[:, 1:] - pos[:, :-1] == 1, axis=1).astype(jnp.int32)
        prefetch += [jnp.min(pos.reshape(bsz, nq, tq), axis=-1), jnp.max(pos.reshape(bsz, nk, tk), axis=-1),
                     consecutive]
        in_specs += [
            pl.BlockSpec((1, tk, tq), lambda b, s, pq, pk, *_: (b, pk[s], pq[s])),
            pl.BlockSpec((1, 1, tq), lambda b, s, pq, pk, *_: (b, 0, pq[s])),
            pl.BlockSpec((1, tk, 1), lambda b, s, pq, pk, *_: (b, pk[s], 0)),
            pl.BlockSpec(btab.shape, lambda b, s, pq, pk, *_: (0, 0)),
            pl.BlockSpec(band.shape, lambda b, s, pq, pk, *_: (0, 0, 0, 0)),
        ]
        args += [mask, pos.reshape(bsz, 1, seq), pos.reshape(bsz, seq, 1), btab, band]
    grid_spec = pltpu.PrefetchScalarGridSpec(
        num_scalar_prefetch=len(prefetch),
        grid=(bsz, len(pq)),
        in_specs=in_specs,
        out_specs=pl.BlockSpec((1, n_heads * dv, tq), lambda b, s, pq, pk, *_: (b, 0, pq[s])),
        scratch_shapes=[pltpu.VMEM((n_heads, tq), F32), pltpu.VMEM((n_heads * dva, tq), F32),
                        pltpu.VMEM((ATTN_S_BUFFERS, ts, tq), F32), pltpu.VMEM((ATTN_P_BUFFERS, ts, tq), BF16)],
    )
    return pl.pallas_call(
        functools.partial(_attn_kernel, n_heads=n_heads, dk=dk, dv=dv, use_mask=use_mask),
        grid_spec=grid_spec,
        out_shape=jax.ShapeDtypeStruct((bsz, n_heads * dv, seq), BF16),
        compiler_params=pltpu.CompilerParams(dimension_semantics=("parallel", "arbitrary"),
                                             vmem_limit_bytes=VMEM_LIMIT),
        name="dsa_attention" if use_mask else "mla_attention",
    )(*prefetch, *args)


def _layer_norm(z, g, b):
    mu = jnp.mean(z, axis=-1, keepdims=True)
    zc = z - mu
    var = jnp.mean(zc * zc, axis=-1, keepdims=True)
    return zc * lax.rsqrt(var + LN_EPS) * g + b


def _merge_kernel(x_ref, yaT_ref, ybT_ref, gate_ref, g1_ref, wa_ref, wb_ref, wo_ref, lg_ref, lb_ref,
                  o_ref, *, alpha):
    d = x_ref.shape[2]
    tn = (((0,), (0,)), ((), ()))
    pa = lax.dot_general(yaT_ref[0], wa_ref[...], tn, preferred_element_type=F32)
    pb = lax.dot_general(ybT_ref[0], wb_ref[...], tn, preferred_element_type=F32)
    gate = gate_ref[0].astype(F32)
    merged = gate[:, :d] * pa + gate[:, d:] * pb
    y = jnp.dot(merged.astype(BF16), wo_ref[...], preferred_element_type=F32)
    o_ref[0] = _layer_norm(alpha * x_ref[0] + g1_ref[0] * y, lg_ref[...], lb_ref[...])


def _merge(x, yaT, ybT, gate, g1, wa, wb, wo, lg, lb, alpha):
    bsz, seq, d = x.shape
    tm = min(TM_PROJ, seq)
    tok = lambda width: pl.BlockSpec((1, tm, width), lambda b, i: (b, i, 0))
    tokT = lambda width: pl.BlockSpec((1, width, tm), lambda b, i: (b, 0, i))
    full = lambda a: pl.BlockSpec(a.shape, lambda b, i: (0,) * a.ndim)
    return pl.pallas_call(
        functools.partial(_merge_kernel, alpha=alpha),
        grid=(bsz, seq // tm),
        in_specs=[tok(d), tokT(yaT.shape[1]), tokT(ybT.shape[1]), tok(2 * d),
                  pl.BlockSpec((1, 1, d), lambda b, i: (b, 0, 0)),
                  full(wa), full(wb), full(wo), full(lg), full(lb)],
        out_specs=tok(d),
        out_shape=jax.ShapeDtypeStruct((bsz, seq, d), F32),
        compiler_params=pltpu.CompilerParams(
            dimension_semantics=("parallel", "parallel"), vmem_limit_bytes=VMEM_LIMIT),
        name="merge_out_ln",
    )(x, yaT, ybT, gate, g1, wa, wb, wo, lg, lb)


def _ffn_kernel(x_ref, sc_ref, sh_ref, g2_ref, wup_ref, cw_ref, cb_ref, wdn_ref, lg_ref, lb_ref,
                o_ref, hist_ref, *, alpha, d_ff):
    i = pl.program_id(1)
    tm = x_ref.shape[1]
    x = x_ref[0]
    h = (x * (1.0 + sc_ref[0]) + sh_ref[0]).astype(BF16)

    @pl.when(i == 0)
    def _():
        hist_ref[...] = jnp.zeros(hist_ref.shape, F32)

    fc = math.gcd(d_ff, FFN_CHUNK)
    row = lax.broadcasted_iota(jnp.int32, (tm, fc), 0)

    def up_conv(c0):
        cols = slice(c0, c0 + fc)
        u = jnp.dot(h, wup_ref[:, cols], preferred_element_type=F32)
        prev = hist_ref[:, cols]
        u1 = jnp.where(row == 0, prev[7:8, :], pltpu.roll(u, 1, axis=0))
        u2 = jnp.where(row == 0, prev[6:7, :], jnp.where(row == 1, prev[7:8, :], pltpu.roll(u, 2, axis=0)))
        hist_ref[:, cols] = u[tm - 8:, :]
        return cw_ref[0:1, cols] * u2 + cw_ref[1:2, cols] * u1 + cw_ref[2:3, cols] * u + cb_ref[:, cols]

    y = None
    for j in range(d_ff // fc):
        g = up_conv(j * fc)
        val = up_conv(d_ff + j * fc)
        a = (g * _sigmoid(g) * val).astype(BF16)
        part = jnp.dot(a, wdn_ref[j * fc:(j + 1) * fc, :], preferred_element_type=F32)
        y = part if y is None else y + part
    o_ref[0] = _layer_norm(alpha * x + g2_ref[0] * y, lg_ref[...], lb_ref[...])


def _ffn(x, sc, sh, g2, wup, cw, cb, wdn, lg, lb, alpha):
    bsz, seq, d = x.shape
    d_ff = wdn.shape[0]
    tm = min(TM_FFN, seq)
    tok = pl.BlockSpec((1, tm, d), lambda b, i: (b, i, 0))
    per_b = pl.BlockSpec((1, 1, d), lambda b, i: (b, 0, 0))
    once = lambda a: pl.BlockSpec(a.shape, lambda b, i: (0,) * a.ndim, pipeline_mode=pl.Buffered(1))
    return pl.pallas_call(
        functools.partial(_ffn_kernel, alpha=alpha, d_ff=d_ff),
        grid=(bsz, seq // tm),
        in_specs=[tok, per_b, per_b, per_b, once(wup), once(cw), once(cb), once(wdn), once(lg), once(lb)],
        out_specs=tok,
        out_shape=jax.ShapeDtypeStruct((bsz, seq, d), F32),
        scratch_shapes=[pltpu.VMEM((8, 2 * d_ff), F32)],
        compiler_params=pltpu.CompilerParams(
            dimension_semantics=("parallel", "arbitrary"), vmem_limit_bytes=VMEM_LIMIT),
        name="conv_ffn_ln",
    )(x, sc, sh, g2, wup, cw, cb, wdn, lg, lb)


def kernel(x, c, positions, rel_bias, w_ada, b_ada, w_in, q_norm_g, w_uq, kv_norm_g, w_ukv,
           w_branch_a, w_branch_b, w_out, ln1_g, ln1_b, w_up, conv_w, conv_b, w_down, ln2_g, ln2_b):
    bsz, seq, d = x.shape
    depth = w_ada.shape[0]
    alpha = (2 * depth) ** 0.25
    topk = min(TOPK_MAX, seq // 4)
    assert seq % TQ == 0 or seq < TQ

    mod = _ada_mod(c, w_ada, b_ada)
    pos3 = positions.reshape(bsz, seq, 1)
    invf = ROPE_THETA ** (-jnp.arange(ROPE_HALF, dtype=F32) * (2.0 / ROPE_DIM))
    invf = jnp.tile(invf, LANES // ROPE_HALF).reshape(1, LANES)
    btab = ((rel_bias[_BUCKET_TABLE] - rel_bias[NUM_BUCKETS - 1][None, :]) * LOG2E).T.astype(F32)
    band = _bias_band(btab)

    for l in range(depth):
        sh1, sc1, g1, sh2, sc2, g2 = [mod[l, :, None, j * d:(j + 1) * d] for j in range(6)]
        qkva, iq, ik, iw, qb, kvb, krb, gate = _inproj(
            x, sc1, sh1, pos3, invf, _pack_w_in(w_in[l], d),
            q_norm_g[l].reshape(1, -1), _pack_w_uq(w_uq[l]),
            kv_norm_g[l].reshape(1, -1), _pack_w_ukv(w_ukv[l]))

        mask = _indexer(iq.transpose(0, 2, 1), iw.transpose(0, 2, 1), ik, topk)
        qaT = qkva[:, :, :A_WIDTH].transpose(0, 2, 1)
        ka = qkva[:, :, A_WIDTH:2 * A_WIDTH]
        vaT = qkva[:, :, 2 * A_WIDTH:].transpose(0, 2, 1)
        yaT = _attention(qaT, ka, vaT, A_HEADS, A_HEAD_DIM, A_HEAD_DIM, mask=mask, pos=positions, btab=btab,
                         band=band)

        n_nope = B_HEADS * NOPE_DIM
        n_r = B_HEADS * ROPE_HALF
        q_heads = jnp.concatenate([
            qb[:, :, :n_nope].reshape(bsz, seq, B_HEADS, NOPE_DIM),
            qb[:, :, n_nope:n_nope + n_r].reshape(bsz, seq, B_HEADS, ROPE_HALF),
            qb[:, :, n_nope + n_r:].reshape(bsz, seq, B_HEADS, ROPE_HALF)], axis=-1)
        qbT = q_heads.reshape(bsz, seq, B_HEADS * B_QK_DIM).transpose(0, 2, 1)
        kr = jnp.concatenate([krb[:, :, :ROPE_HALF], krb[:, :, LANES:LANES + ROPE_HALF]], axis=-1)
        k_heads = jnp.concatenate([
            kvb[:, :, :n_nope].reshape(bsz, seq, B_HEADS, NOPE_DIM),
            jnp.broadcast_to(kr[:, :, None, :], (bsz, seq, B_HEADS, ROPE_DIM))], axis=-1)
        kb = k_heads.reshape(bsz, seq, B_HEADS * B_QK_DIM)
        vbT = kvb[:, :, n_nope:].transpose(0, 2, 1)
        ybT = _attention(qbT, kb, vbT, B_HEADS, B_QK_DIM, V_DIM)

        x = _merge(x, yaT, ybT, gate, g1,
                   w_branch_a[l].astype(BF16), w_branch_b[l].astype(BF16), w_out[l].astype(BF16),
                   ln1_g[l].reshape(1, d), ln1_b[l].reshape(1, d), alpha)
        x = _ffn(x, sc2, sh2, g2, w_up[l].astype(BF16), conv_w[l], conv_b[l].reshape(1, -1),
                 w_down[l].astype(BF16), ln2_g[l].reshape(1, d), ln2_b[l].reshape(1, d), alpha)
    return x
```

```python
import functools
import math

import numpy as np
import jax
import jax.numpy as jnp
from jax import lax
from jax.experimental import pallas as pl
from jax.experimental.pallas import tpu as pltpu

F32 = jnp.float32
BF16 = jnp.bfloat16

A_HEADS = 8
A_HEAD_DIM = 64
IDX_HEADS = 8
IDX_DIM = 64
TOPK_MAX = 256
B_HEADS = 8
Q_RANK = 256
KV_RANK = 128
NOPE_DIM = 64
ROPE_DIM = 32
ROPE_HALF = ROPE_DIM // 2
V_DIM = 64
ROPE_THETA = 10000.0
NUM_BUCKETS = 32
MAX_DISTANCE = 128
CONV_WIDTH = 3
LN_EPS = 1e-5
RMS_EPS = 1e-6
A_WIDTH = A_HEADS * A_HEAD_DIM
B_WIDTH = B_HEADS * V_DIM
B_QK_DIM = NOPE_DIM + ROPE_DIM

LANES = 128
V7X_VMEM_BYTES = 64 * 1024 * 1024
VMEM_LIMIT = 52 * 1024 * 1024

NEG = -1e30
LOG2E = math.log2(math.e)

TM_PROJ = 512
TM_FFN = 512
FFN_CHUNK = 1408
TQ = 256
TK = 1024
IDX_CHUNK = 256
IDX_SCORE_CHUNK = 256
IDX_SCORE_GROUP = 4
IDX_COUNT_GROUP = 4
ATTN_CHUNK = 128
ONES_ROWS = 16
ATTN_P_BUFFERS = 2
ATTN_SUB, ATTN_S_BUFFERS = 256, 6
ATTN_SUB_MASKED, ATTN_S_BUFFERS_MASKED = 512, 4


def _t5_bucket_table():
    n = np.arange(MAX_DISTANCE)
    max_exact = NUM_BUCKETS // 2
    out = []
    for dt in (np.float32, np.float64):
        ratio = np.log(np.maximum(n, 1).astype(dt) / dt(max_exact)) / dt(math.log(MAX_DISTANCE / max_exact))
        large = max_exact + (ratio * dt(NUM_BUCKETS - max_exact)).astype(np.int32)
        large = np.minimum(large, NUM_BUCKETS - 1)
        out.append(np.where(n < max_exact, n, large))
    assert (out[0] == out[1]).all()
    assert out[0][-1] == NUM_BUCKETS - 1
    return out[0].astype(np.int32)


_BUCKET_TABLE = _t5_bucket_table()


def _ada_kernel(c_ref, w_ref, b_ref, o_ref):
    c = c_ref[...]
    s = c * (1.0 / (1.0 + jnp.exp(-c)))
    o_ref[0] = jnp.dot(s, w_ref[0], preferred_element_type=F32) + b_ref[0]


def _ada_mod(c, w_ada, b_ada):
    depth, d, d6 = w_ada.shape
    bsz = c.shape[0]
    n_chunks = d6 // d
    return pl.pallas_call(
        _ada_kernel,
        grid=(depth, n_chunks),
        in_specs=[
            pl.BlockSpec((bsz, d), lambda l, j: (0, 0)),
            pl.BlockSpec((1, d, d), lambda l, j: (l, 0, j)),
            pl.BlockSpec((1, 1, d), lambda l, j: (l, 0, j)),
        ],
        out_specs=pl.BlockSpec((1, bsz, d), lambda l, j: (l, 0, j)),
        out_shape=jax.ShapeDtypeStruct((depth, bsz, d6), F32),
        name="ada_mod",
    )(c, w_ada, b_ada.reshape(depth, 1, d6))


_C_QKV = 0
_C_IQ = _C_QKV + 3 * A_WIDTH
_C_IK = _C_IQ + IDX_HEADS * IDX_DIM
_C_IW = _C_IK + LANES
_C_CQ = _C_IW + LANES
_C_CKV = _C_CQ + Q_RANK
_C_KR1 = _C_CKV + KV_RANK
_C_KR2 = _C_KR1 + LANES
_C_GATE = _C_KR2 + LANES


def _pack_w_in(w_in, d):
    sizes = (A_WIDTH, A_WIDTH, A_WIDTH, IDX_HEADS * IDX_DIM, IDX_DIM, IDX_HEADS,
             Q_RANK, KV_RANK, ROPE_DIM, d, d)
    offs = np.cumsum((0,) + sizes)
    seg = {name: w_in[:, offs[i]:offs[i + 1]] for i, name in enumerate(
        ("qa", "ka", "va", "iq", "ik", "iw", "cq", "ckv", "kr", "ga", "gb"))}

    def pad(w, width):
        return jnp.pad(w, ((0, 0), (0, width - w.shape[1])))

    parts = [seg["qa"], seg["ka"], seg["va"], seg["iq"],
             pad(seg["ik"], LANES), pad(seg["iw"], LANES),
             seg["cq"], seg["ckv"],
             pad(seg["kr"][:, :ROPE_HALF], LANES), pad(seg["kr"][:, ROPE_HALF:], LANES),
             seg["ga"], seg["gb"]]
    return jnp.concatenate(parts, axis=1).astype(BF16)


def _pack_w_uq(w_uq):
    w = w_uq.reshape(Q_RANK, B_HEADS, B_QK_DIM)
    nope = w[:, :, :NOPE_DIM].reshape(Q_RANK, B_HEADS * NOPE_DIM)
    r1 = w[:, :, NOPE_DIM:NOPE_DIM + ROPE_HALF].reshape(Q_RANK, B_HEADS * ROPE_HALF)
    r2 = w[:, :, NOPE_DIM + ROPE_HALF:].reshape(Q_RANK, B_HEADS * ROPE_HALF)
    return jnp.concatenate([nope, r1, r2], axis=1).astype(BF16)


def _pack_w_ukv(w_ukv):
    w = w_ukv.reshape(KV_RANK, B_HEADS, NOPE_DIM + V_DIM)
    kn = w[:, :, :NOPE_DIM].reshape(KV_RANK, B_HEADS * NOPE_DIM)
    v = w[:, :, NOPE_DIM:].reshape(KV_RANK, B_HEADS * V_DIM)
    return jnp.concatenate([kn, v], axis=1).astype(BF16)


def _sigmoid(x):
    return 1.0 / (1.0 + jnp.exp(-x))


def _rms(x, g):
    return x * lax.rsqrt(jnp.mean(x * x, axis=-1, keepdims=True) + RMS_EPS) * g


def _inproj_kernel(x_ref, sc_ref, sh_ref, pos_ref, invf_ref, w_ref, qg_ref, wuq_ref, kvg_ref, wukv_ref,
                   o_qkva, o_iq, o_ik, o_iw, o_qb, o_kvb, o_krb, o_gate, *, d_model):
    h = (x_ref[0] * (1.0 + sc_ref[0]) + sh_ref[0]).astype(BF16)

    def proj(c0, width):
        return jnp.dot(h, w_ref[:, c0:c0 + width], preferred_element_type=F32)

    a_scale = A_HEAD_DIM ** -0.5 * LOG2E
    o_qkva[0, :, 0:A_WIDTH] = (proj(_C_QKV, A_WIDTH) * a_scale).astype(BF16)
    o_qkva[0, :, A_WIDTH:3 * A_WIDTH] = proj(_C_QKV + A_WIDTH, 2 * A_WIDTH).astype(BF16)
    o_iq[0] = proj(_C_IQ, IDX_HEADS * IDX_DIM).astype(BF16)
    o_ik[0] = proj(_C_IK, LANES)[:, :IDX_DIM].astype(BF16)
    o_iw[0] = proj(_C_IW, LANES)[:, :IDX_HEADS] * (IDX_DIM ** -0.5 * IDX_HEADS ** -0.5)

    ang = pos_ref[0].astype(F32) * invf_ref[...]
    cos, sin = jnp.cos(ang), jnp.sin(ang)

    b_scale = B_QK_DIM ** -0.5 * LOG2E
    cq = _rms(proj(_C_CQ, Q_RANK), qg_ref[...]).astype(BF16)
    q = jnp.dot(cq, wuq_ref[...], preferred_element_type=F32)
    n_nope = B_HEADS * NOPE_DIM
    n_r = B_HEADS * ROPE_HALF
    x1 = q[:, n_nope:n_nope + n_r]
    x2 = q[:, n_nope + n_r:]
    o_qb[0, :, 0:n_nope] = (q[:, :n_nope] * b_scale).astype(BF16)
    o_qb[0, :, n_nope:n_nope + n_r] = ((x1 * cos - x2 * sin) * b_scale).astype(BF16)
    o_qb[0, :, n_nope + n_r:] = ((x1 * sin + x2 * cos) * b_scale).astype(BF16)

    ckv = _rms(proj(_C_CKV, KV_RANK), kvg_ref[...]).astype(BF16)
    o_kvb[0] = jnp.dot(ckv, wukv_ref[...], preferred_element_type=F32).astype(BF16)

    k1 = proj(_C_KR1, LANES)
    k2 = proj(_C_KR2, LANES)
    o_krb[0, :, 0:LANES] = (k1 * cos - k2 * sin).astype(BF16)
    o_krb[0, :, LANES:] = (k1 * sin + k2 * cos).astype(BF16)

    o_gate[0] = _sigmoid(proj(_C_GATE, 2 * d_model)).astype(BF16)


def _inproj(x, sc, sh, pos3, invf, w_pack, qg, wuq, kvg, wukv):
    bsz, seq, d = x.shape
    tm = min(TM_PROJ, seq)
    grid = (bsz, seq // tm)
    tok = lambda width: pl.BlockSpec((1, tm, width), lambda b, i: (b, i, 0))
    per_b = pl.BlockSpec((1, 1, d), lambda b, i: (b, 0, 0))
    full = lambda a: pl.BlockSpec(a.shape, lambda b, i: (0,) * a.ndim)
    out_widths = (3 * A_WIDTH, IDX_HEADS * IDX_DIM, IDX_DIM, IDX_HEADS,
                  B_HEADS * B_QK_DIM, B_HEADS * (NOPE_DIM + V_DIM), 2 * LANES, 2 * d)
    out_dtypes = (BF16, BF16, BF16, F32, BF16, BF16, BF16, BF16)
    return pl.pallas_call(
        functools.partial(_inproj_kernel, d_model=d),
        grid=grid,
        in_specs=[tok(d), per_b, per_b, tok(1), full(invf), full(w_pack), full(qg), full(wuq),
                  full(kvg), full(wukv)],
        out_specs=[tok(w) for w in out_widths],
        out_shape=[jax.ShapeDtypeStruct((bsz, seq, w), dt) for w, dt in zip(out_widths, out_dtypes)],
        compiler_params=pltpu.CompilerParams(
            dimension_semantics=("parallel", "parallel"), vmem_limit_bytes=VMEM_LIMIT),
        name="in_proj",
    )(x, sc, sh, pos3, invf, w_pack, qg, wuq, kvg, wukv)


def _ordered_key(x):
    b = lax.bitcast_convert_type(x, jnp.int32)
    return jnp.where(b < 0, b ^ jnp.int32(0x7FFFFFFF), b)


def _from_ordered_key(k):
    b = jnp.where(k < 0, k ^ jnp.int32(0x7FFFFFFF), k)
    return lax.bitcast_convert_type(b, F32)


def _indexer_kernel(iqT_ref, iwT_ref, ik_ref, o_ref, tcnt_ref, *, topk, seq):
    tq = o_ref.shape[2]
    ch = IDX_CHUNK
    sch = min(IDX_SCORE_CHUNK, ch)
    qi = pl.program_id(1)
    n_chunks = (qi * tq) // ch + tq // ch
    sub = 8
    kf = float(topk)

    def fold(v, op):
        return op(v.reshape(v.shape[0] // sub, sub, tq), axis=0)

    def count(pred):
        return fold(jnp.where(pred, 1.0, 0.0), jnp.sum)

    def chunk(c):
        k0 = pl.multiple_of(c * ch, ch)
        return k0, o_ref[0, pl.ds(k0, ch), :]

    def score_chunk(c, carry, diagonal):
        vmax, vmin, n_pos, n_nonneg = carry
        k0 = pl.multiple_of(c * sch, sch)
        ik = ik_ref[0, pl.ds(k0, sch), :]
        acc = jnp.zeros((sch, tq), F32)
        for hd in range(IDX_HEADS):
            y = jnp.dot(ik, iqT_ref[0, hd * IDX_DIM:(hd + 1) * IDX_DIM, :],
                        preferred_element_type=F32)
            acc = acc + jnp.maximum(y, 0.0) * iwT_ref[0, hd:hd + 1, :]
        if diagonal:
            key_idx = k0 + lax.broadcasted_iota(jnp.int32, (sch, tq), 0)
            q_idx = qi * tq + lax.broadcasted_iota(jnp.int32, (sch, tq), 1)
            causal = key_idx <= q_idx
            acc = jnp.where(causal, acc, -jnp.inf)
            vmin = jnp.minimum(vmin, fold(jnp.where(causal, acc, jnp.inf), jnp.min))
        else:
            vmin = jnp.minimum(vmin, fold(acc, jnp.min))
        o_ref[0, pl.ds(k0, sch), :] = acc
        vmax = jnp.maximum(vmax, fold(acc, jnp.max))
        n_pos = n_pos + count(acc > 0.0)
        n_nonneg = n_nonneg + count(acc >= 0.0)
        return vmax, vmin, n_pos, n_nonneg

    n_score = n_chunks * (ch // sch)
    n_diag = tq // sch
    n_full = n_score - n_diag
    sg = IDX_SCORE_GROUP

    def score_group(g, carry):
        for i in range(sg):
            carry = score_chunk(g * sg + i, carry, diagonal=False)
        return carry

    stats = (jnp.full((sub, tq), -jnp.inf, F32), jnp.full((sub, tq), jnp.inf, F32),
             jnp.zeros((sub, tq), F32), jnp.zeros((sub, tq), F32))
    stats = lax.fori_loop(0, n_full // sg, score_group, stats)
    stats = lax.fori_loop((n_full // sg) * sg, n_full, functools.partial(score_chunk, diagonal=False), stats)
    stats = lax.fori_loop(n_full, n_score, functools.partial(score_chunk, diagonal=True), stats)
    row_max = jnp.max(stats[0], axis=0, keepdims=True)
    row_min = jnp.min(stats[1], axis=0, keepdims=True)
    n_pos = jnp.sum(stats[2], axis=0, keepdims=True)
    n_nonneg = jnp.sum(stats[3], axis=0, keepdims=True)

    group = math.gcd(seq // ch, IDX_COUNT_GROUP)
    n_steps = (n_chunks + group - 1) // group

    def pad_chunk(c, carry):
        o_ref[0, pl.ds(pl.multiple_of(c * ch, ch), ch), :] = jnp.full((ch, tq), -jnp.inf, F32)
        return carry
    lax.fori_loop(n_chunks, n_steps * group, pad_chunk, 0)

    def count_ge(thr):
        def body(c, parts):
            return tuple(p + count(chunk(c * group + i)[1] >= thr) for i, p in enumerate(parts))
        parts = lax.fori_loop(0, n_steps, body, (jnp.zeros((sub, tq), F32),) * group)
        return jnp.sum(functools.reduce(jnp.add, parts), axis=0, keepdims=True)

    n_valid = (qi * tq + lax.broadcasted_iota(jnp.int32, (1, tq), 1) + 1).astype(F32)

    keep_all = n_valid <= kf
    zero_tie = jnp.logical_not(keep_all) & (n_pos < kf) & (n_nonneg >= kf)
    positive = n_pos >= kf
    lo_key0 = jnp.where(positive, 1, _ordered_key(row_min))
    hi_key0 = jnp.where(positive, _ordered_key(row_max) + 1, -2)
    n_hi0 = jnp.where(positive, 0.0, n_nonneg)
    settled = keep_all | zero_tie
    short0 = jnp.logical_not(settled) & (n_hi0 == kf - 1.0)
    done0 = jnp.where(settled | short0, 1, 0).astype(jnp.int32)
    thr0 = jnp.where(keep_all, -jnp.inf, jnp.where(zero_tie, 0.0, row_min))
    tie0 = jnp.where(zero_tie, 1, 0).astype(jnp.int32)

    def n_active(done):
        return jnp.sum(jnp.where(done > 0, 0.0, 1.0))

    def cond(st):
        return st[0] > 0.0

    def body(st):
        _, it, lo_key, hi_key, n_hi, thr, done, tie, short = st
        lo = _from_ordered_key(lo_key)
        hi = _from_ordered_key(hi_key)
        mid_val_key = _ordered_key(lo * 0.5 + hi * 0.5)
        mid_bit_key = (lo_key & hi_key) + ((lo_key ^ hi_key) >> 1)
        use_val = (it < 4) & (jnp.abs(hi) < jnp.inf) & (jnp.abs(lo) < jnp.inf)
        mid_key = jnp.where(use_val, mid_val_key, mid_bit_key)
        mid_key = jnp.minimum(jnp.maximum(mid_key, lo_key + 1), hi_key - 1)
        mid = _from_ordered_key(mid_key)
        cnt = count_ge(mid)
        active = done == 0
        ge = cnt >= kf
        lo_key = jnp.where(active & ge, mid_key, lo_key)
        hi_key = jnp.where(active & jnp.logical_not(ge), mid_key, hi_key)
        n_hi = jnp.where(active & jnp.logical_not(ge), cnt, n_hi)
        hit = active & (cnt == kf)
        stuck = active & jnp.logical_not(hit) & (hi_key - 1 <= lo_key)
        one_short = active & jnp.logical_not(hit | stuck) & (n_hi == kf - 1.0)
        thr = jnp.where(hit, mid, jnp.where(stuck, _from_ordered_key(lo_key), thr))
        tie = jnp.where(stuck, 1, tie)
        short = jnp.where(one_short, 1, short)
        done = jnp.where(hit | stuck | one_short, 1, done)
        return n_active(done), it + 1, lo_key, hi_key, n_hi, thr, done, tie, short

    st = lax.while_loop(cond, body, (n_active(done0), jnp.int32(0), lo_key0, hi_key0, n_hi0, thr0, done0, tie0,
                                     jnp.where(short0, 1, 0).astype(jnp.int32)))
    thr, tie, short = st[5], st[7] > 0, st[8] > 0
    hi_end = _from_ordered_key(st[3])

    def max_below_hi():
        def body(c, parts):
            return tuple(jnp.maximum(p, fold(jnp.where(v < hi_end, v, -jnp.inf), jnp.max))
                         for p, v in ((p, chunk(c * group + i)[1]) for i, p in enumerate(parts)))
        parts = lax.fori_loop(0, n_steps, body, (jnp.full((sub, tq), -jnp.inf, F32),) * group)
        return jnp.max(functools.reduce(jnp.maximum, parts), axis=0, keepdims=True)

    n_short = jnp.sum(jnp.where(short, 1.0, 0.0))
    thr = jnp.where(short, lax.cond(n_short > 0.0, max_below_hi, lambda: thr), thr)
    tie = tie | short
    n_tie = jnp.sum(jnp.where(tie, 1.0, 0.0))

    def write_mask(k0, sel):
        o_ref[0, pl.ds(k0, ch), :] = jnp.where(sel, 0.0, NEG)

    def grouped(per_chunk):
        def body(g, carry):
            for i in range(group):
                carry = per_chunk(g * group + i, carry)
            return carry
        return body

    f32_info = jnp.finfo(F32)
    cut_incl = jnp.where(thr == -jnp.inf, float(f32_info.min), thr)
    above = _from_ordered_key(_ordered_key(thr) + 1)
    cut_excl = jnp.where((above >= 0.0) & (above < float(f32_info.tiny)), float(f32_info.tiny), above)

    def mask_plain():
        def body(c, carry):
            k0, v = chunk(c)
            write_mask(k0, v >= cut_incl)
            return carry
        lax.fori_loop(0, n_steps, grouped(body), 0)

    def mask_ties():
        def count_body(c, n_gt):
            _, v = chunk(c)
            tcnt_ref[c] = count(v == thr)
            return n_gt + count(v > thr)
        n_gt = lax.fori_loop(0, n_steps, grouped(count_body), jnp.zeros((sub, tq), F32))
        n_gt = jnp.sum(n_gt, axis=0, keepdims=True)
        need = jnp.where(tie, kf - n_gt, float(seq))

        def quota_body(c, carry):
            before, split = carry
            t = jnp.sum(tcnt_ref[c], axis=0, keepdims=True)
            keep = jnp.clip(need - before, 0.0, t)
            tcnt_ref[c, 0:1, :] = keep
            split = jnp.where((keep > 0.0) & (keep < t), c.astype(F32), split)
            return before + t, split
        _, split0 = lax.fori_loop(0, n_steps * group, quota_body,
                                  (jnp.zeros((1, tq), F32), jnp.full((1, tq), -1.0, F32)))

        def split_body(st):
            c_f, split = st
            c = c_f.astype(jnp.int32)
            k0, v = chunk(c)
            is_tie = v == thr
            r = lax.broadcasted_iota(jnp.int32, (ch, ch), 0)
            s = lax.broadcasted_iota(jnp.int32, (ch, ch), 1)
            lower = jnp.where(s < r, 1.0, 0.0).astype(BF16)
            rank = jnp.dot(lower, jnp.where(is_tie, 1.0, 0.0).astype(BF16), preferred_element_type=F32)
            o_ref[0, pl.ds(k0, ch), :] = jnp.where(is_tie & (rank >= tcnt_ref[c, 0:1, :]), -jnp.inf, v)
            split = jnp.where(split == c_f, -1.0, split)
            return jnp.max(split), split
        lax.while_loop(lambda st: st[0] >= 0.0, split_body, (jnp.max(split0), split0))

        def body(c, carry):
            k0, v = chunk(c)
            keep_ties = tcnt_ref[c, 0:1, :] > 0.0
            write_mask(k0, v >= jnp.where(keep_ties, cut_incl, cut_excl))
            return carry
        lax.fori_loop(0, n_steps, grouped(body), 0)

    lax.cond(n_tie > 0.0, mask_ties, mask_plain)

    def fill_chunk(c, carry):
        k0 = pl.multiple_of(c * ch, ch)
        o_ref[0, pl.ds(k0, ch), :] = jnp.full((ch, tq), NEG, F32)
        return carry

    lax.fori_loop(n_chunks, seq // ch, fill_chunk, 0)


def _indexer(iqT, iwT, ik, topk):
    bsz, _, seq = iqT.shape
    tq = min(TQ, seq)
    return pl.pallas_call(
        functools.partial(_indexer_kernel, topk=topk, seq=seq),
        grid=(bsz, seq // tq),
        in_specs=[
            pl.BlockSpec((1, IDX_HEADS * IDX_DIM, tq), lambda b, i: (b, 0, i)),
            pl.BlockSpec((1, IDX_HEADS, tq), lambda b, i: (b, 0, i)),
            pl.BlockSpec((1, seq, IDX_DIM), lambda b, i: (b, 0, 0)),
        ],
        out_specs=pl.BlockSpec((1, seq, tq), lambda b, i: (b, 0, i)),
        out_shape=jax.ShapeDtypeStruct((bsz, seq, seq), F32),
        scratch_shapes=[pltpu.VMEM((max(seq // IDX_CHUNK, 1), 8, tq), F32)],
        compiler_params=pltpu.CompilerParams(
            dimension_semantics=("parallel", "parallel"), vmem_limit_bytes=VMEM_LIMIT),
        name="dsa_indexer",
    )(iqT, iwT, ik)


def _attn_kernel(*refs, n_heads, dk, dv, use_mask):
    if use_mask:
        (pq_ref, pk_ref, qmin_ref, kmax_ref, reg_ref, qT_ref, k_ref, vT_ref, mask_ref, qpos_ref, kpos_ref,
         btab_ref, band_ref, o_ref, m_sc, acc_sc, s_buf, p_buf) = refs
    else:
        pq_ref, pk_ref, qT_ref, k_ref, vT_ref, o_ref, m_sc, acc_sc, s_buf, p_buf = refs
    tq = qT_ref.shape[2]
    tk = k_ref.shape[1]
    b = pl.program_id(0)
    pair = pl.program_id(1)
    qi = pq_ref[pair]
    ki = pk_ref[pair]
    last_k = ((qi + 1) * tq - 1) // tk

    @pl.when(ki == 0)
    def _init():
        m_sc[...] = jnp.full(m_sc.shape, NEG, F32)
        acc_sc[...] = jnp.zeros(acc_sc.shape, F32)

    ts = s_buf.shape[1]
    ck = min(ATTN_CHUNK, ts)
    n_chunks = ts // ck
    dva = vT_ref.shape[1] // n_heads
    rows = lambda g: slice(g * ck, (g + 1) * ck)

    n_sub = tk // ts
    n_live = jnp.minimum(((qi + 1) * tq - ki * tk + ts - 1) // ts, n_sub)

    def step(extra, live=n_sub):
        n_buf = s_buf.shape[0]
        n_pbuf = p_buf.shape[0]
        ahead = n_buf - 1
        units = [(j, h) for j in range(live) for h in range(n_heads)]

        def logits_to_buf(i):
            j, h = units[i]
            part = None
            for c in range(n_chunks):
                g = j * n_chunks + c
                s = jnp.dot(k_ref[0, rows(g), h * dk:(h + 1) * dk], qT_ref[0, h * dk:(h + 1) * dk, :],
                            preferred_element_type=F32)
                s = extra(h, g, s)
                s_buf[i % n_buf, c * ck:(c + 1) * ck, :] = s
                part = s if part is None else jnp.maximum(part, s)
            return jnp.max(part, axis=0, keepdims=True)

        tile_max = {i: logits_to_buf(i) for i in range(min(ahead, len(units)))}
        for i, (j, h) in enumerate(units):
            m_prev = m_sc[h:h + 1, :]
            m_new = jnp.maximum(m_prev, tile_max.pop(i))
            alpha = jnp.exp2(m_prev - m_new)
            if i + ahead < len(units):
                tile_max[i + ahead] = logits_to_buf(i + ahead)
            p_buf[i % n_pbuf] = jnp.exp2(s_buf[i % n_buf] - m_new).astype(BF16)
            pv = jnp.dot(vT_ref[0, h * dva:(h + 1) * dva, j * ts:(j + 1) * ts], p_buf[i % n_pbuf],
                         preferred_element_type=F32)
            acc_sc[h * dva:(h + 1) * dva, :] = acc_sc[h * dva:(h + 1) * dva, :] * alpha + pv
            m_sc[h:h + 1, :] = m_new

    if use_mask:
        near = qmin_ref[b, qi] - kmax_ref[b, ki] < MAX_DISTANCE

        @pl.when(jnp.logical_not(near))
        def _far():
            step(lambda h, c, s: s + mask_ref[0, rows(c), :])

        consecutive = reg_ref[b] > 0

        def banded(h, c, s):
            parts = []
            for j in range(tq // LANES):
                dist = (qi * tq + j * LANES) - (ki * tk + c * ck)
                parts.append(jnp.where(dist == 0, band_ref[h, 0],
                                       jnp.where(dist == ck, band_ref[h, 1], 0.0)))
            return s + mask_ref[0, rows(c), :] + jnp.concatenate(parts, axis=1)

        for live in range(1, n_sub + 1):
            pl.when(near & consecutive & (n_live == live))(functools.partial(step, banded, live))

        @pl.when(near & jnp.logical_not(consecutive))
        def _near_general():
            def extra(h, c, s):
                rel = qpos_ref[0] - kpos_ref[0, rows(c), :]
                n = jnp.clip(rel, 0, MAX_DISTANCE - 1)
                tab = jnp.broadcast_to(btab_ref[h:h + 1, :], (ck, LANES))
                parts = [jnp.take_along_axis(tab, n[:, j * LANES:(j + 1) * LANES], axis=1)
                         for j in range(tq // LANES)]
                return s + mask_ref[0, rows(c), :] + jnp.concatenate(parts, axis=1)
            step(extra)
    else:
        diag = (ki + 1) * tk > qi * tq

        @pl.when(jnp.logical_not(diag))
        def _full():
            step(lambda h, c, s: s)

        def causal(h, c, s):
            key_idx = ki * tk + c * ck + lax.broadcasted_iota(jnp.int32, (ck, tq), 0)
            q_idx = qi * tq + lax.broadcasted_iota(jnp.int32, (ck, tq), 1)
            return jnp.where(key_idx <= q_idx, s, NEG)

        for live in range(1, n_sub + 1):
            pl.when(diag & (n_live == live))(functools.partial(step, causal, live))

    @pl.when(ki == last_k)
    def _fin():
        for h in range(n_heads):
            inv = 1.0 / acc_sc[h * dva + dv:h * dva + dv + 1, :]
            o_ref[0, h * dv:(h + 1) * dv, :] = (acc_sc[h * dva:h * dva + dv, :] * inv).astype(o_ref.dtype)


def _causal_pairs(nq, tq, tk):
    pq, pk = [], []
    for qi in range(nq):
        for ki in range(((qi + 1) * tq - 1) // tk + 1):
            pq.append(qi)
            pk.append(ki)
    return np.asarray(pq, np.int32), np.asarray(pk, np.int32)


def _bias_band(btab):
    off = np.arange(LANES)
    dist = off[None, :] - off[:, None]
    idx = np.stack([np.clip(dist, 0, MAX_DISTANCE - 1), np.clip(dist + LANES, 0, MAX_DISTANCE - 1)])
    return btab[:, idx]


def _attention(qT, k, vT, n_heads, dk, dv, mask=None, pos=None, btab=None, band=None):
    bsz, _, seq = qT.shape
    dva = dv + ONES_ROWS
    vT = jnp.concatenate([vT.reshape(bsz, n_heads, dv, seq),
                          jnp.ones((bsz, n_heads, ONES_ROWS, seq), vT.dtype)], axis=2)
    vT = vT.reshape(bsz, n_heads * dva, seq)
    tq = min(TQ, seq)
    tk = min(TK, seq)
    use_mask = mask is not None
    sub_keys, s_buffers = (ATTN_SUB_MASKED, ATTN_S_BUFFERS_MASKED) if use_mask else (ATTN_SUB, ATTN_S_BUFFERS)
    ts = min(sub_keys, tk)
    nq, nk = seq // tq, seq // tk
    pq, pk = _causal_pairs(nq, tq, tk)

    in_specs = [
        pl.BlockSpec((1, n_heads * dk, tq), lambda b, s, pq, pk, *_: (b, 0, pq[s])),
        pl.BlockSpec((1, tk, n_heads * dk), lambda b, s, pq, pk, *_: (b, pk[s], 0)),
        pl.BlockSpec((1, n_heads * dva, tk), lambda b, s, pq, pk, *_: (b, 0, pk[s])),
    ]
    args = [qT, k, vT]
    prefetch = [jnp.asarray(pq), jnp.asarray(pk)]
    if use_mask:
        assert min(ATTN_CHUNK, ts) == LANES
        consecutive = jnp.all(pos[:, 1:] - pos[:, :-1] == 1, axis=1).astype(jnp.int32)
        prefetch += [jnp.min(pos.reshape(bsz, nq, tq), axis=-1), jnp.max(pos.reshape(bsz, nk, tk), axis=-1),
                     consecutive]
        in_specs += [
            pl.BlockSpec((1, tk, tq), lambda b, s, pq, pk, *_: (b, pk[s], pq[s])),
            pl.BlockSpec((1, 1, tq), lambda b, s, pq, pk, *_: (b, 0, pq[s])),
            pl.BlockSpec((1, tk, 1), lambda b, s, pq, pk, *_: (b, pk[s], 0)),
            pl.BlockSpec(btab.shape, lambda b, s, pq, pk, *_: (0, 0)),
            pl.BlockSpec(band.shape, lambda b, s, pq, pk, *_: (0, 0, 0, 0)),
        ]
        args += [mask, pos.reshape(bsz, 1, seq), pos.reshape(bsz, seq, 1), btab, band]
    grid_spec = pltpu.PrefetchScalarGridSpec(
        num_scalar_prefetch=len(prefetch),
        grid=(bsz, len(pq)),
        in_specs=in_specs,
        out_specs=pl.BlockSpec((1, n_heads * dv, tq), lambda b, s, pq, pk, *_: (b, 0, pq[s])),
        scratch_shapes=[pltpu.VMEM((n_heads, tq), F32), pltpu.VMEM((n_heads * dva, tq), F32),
                        pltpu.VMEM((s_buffers, ts, tq), F32), pltpu.VMEM((ATTN_P_BUFFERS, ts, tq), BF16)],
    )
    return pl.pallas_call(
        functools.partial(_attn_kernel, n_heads=n_heads, dk=dk, dv=dv, use_mask=use_mask),
        grid_spec=grid_spec,
        out_shape=jax.ShapeDtypeStruct((bsz, n_heads * dv, seq), BF16),
        compiler_params=pltpu.CompilerParams(dimension_semantics=("parallel", "arbitrary"),
                                             vmem_limit_bytes=VMEM_LIMIT),
        name="dsa_attention" if use_mask else "mla_attention",
    )(*prefetch, *args)


def _layer_norm(z, g, b):
    mu = jnp.mean(z, axis=-1, keepdims=True)
    zc = z - mu
    var = jnp.mean(zc * zc, axis=-1, keepdims=True)
    return zc * lax.rsqrt(var + LN_EPS) * g + b


def _merge_kernel(x_ref, yaT_ref, ybT_ref, gate_ref, g1_ref, wa_ref, wb_ref, wo_ref, lg_ref, lb_ref,
                  o_ref, *, alpha):
    d = x_ref.shape[2]
    tn = (((0,), (0,)), ((), ()))
    pa = lax.dot_general(yaT_ref[0], wa_ref[...], tn, preferred_element_type=F32)
    pb = lax.dot_general(ybT_ref[0], wb_ref[...], tn, preferred_element_type=F32)
    gate = gate_ref[0].astype(F32)
    merged = gate[:, :d] * pa + gate[:, d:] * pb
    y = jnp.dot(merged.astype(BF16), wo_ref[...], preferred_element_type=F32)
    o_ref[0] = _layer_norm(alpha * x_ref[0] + g1_ref[0] * y, lg_ref[...], lb_ref[...])


def _merge(x, yaT, ybT, gate, g1, wa, wb, wo, lg, lb, alpha):
    bsz, seq, d = x.shape
    tm = min(TM_PROJ, seq)
    tok = lambda width: pl.BlockSpec((1, tm, width), lambda b, i: (b, i, 0))
    tokT = lambda width: pl.BlockSpec((1, width, tm), lambda b, i: (b, 0, i))
    full = lambda a: pl.BlockSpec(a.shape, lambda b, i: (0,) * a.ndim)
    return pl.pallas_call(
        functools.partial(_merge_kernel, alpha=alpha),
        grid=(bsz, seq // tm),
        in_specs=[tok(d), tokT(yaT.shape[1]), tokT(ybT.shape[1]), tok(2 * d),
                  pl.BlockSpec((1, 1, d), lambda b, i: (b, 0, 0)),
                  full(wa), full(wb), full(wo), full(lg), full(lb)],
        out_specs=tok(d),
        out_shape=jax.ShapeDtypeStruct((bsz, seq, d), F32),
        compiler_params=pltpu.CompilerParams(
            dimension_semantics=("parallel", "parallel"), vmem_limit_bytes=VMEM_LIMIT),
        name="merge_out_ln",
    )(x, yaT, ybT, gate, g1, wa, wb, wo, lg, lb)


def _ffn_kernel(x_ref, sc_ref, sh_ref, g2_ref, wup_ref, cw_ref, cb_ref, wdn_ref, lg_ref, lb_ref,
                o_ref, hist_ref, *, alpha, d_ff):
    i = pl.program_id(1)
    tm = x_ref.shape[1]
    x = x_ref[0]
    h = (x * (1.0 + sc_ref[0]) + sh_ref[0]).astype(BF16)

    @pl.when(i == 0)
    def _():
        hist_ref[...] = jnp.zeros(hist_ref.shape, F32)

    fc = math.gcd(d_ff, FFN_CHUNK)
    row = lax.broadcasted_iota(jnp.int32, (tm, fc), 0)

    def up_conv(c0):
        cols = slice(c0, c0 + fc)
        u = jnp.dot(h, wup_ref[:, cols], preferred_element_type=F32)
        prev = hist_ref[:, cols]
        u1 = jnp.where(row == 0, prev[7:8, :], pltpu.roll(u, 1, axis=0))
        u2 = jnp.where(row == 0, prev[6:7, :], jnp.where(row == 1, prev[7:8, :], pltpu.roll(u, 2, axis=0)))
        hist_ref[:, cols] = u[tm - 8:, :]
        return cw_ref[0:1, cols] * u2 + cw_ref[1:2, cols] * u1 + cw_ref[2:3, cols] * u + cb_ref[:, cols]

    y = None
    for j in range(d_ff // fc):
        g = up_conv(j * fc)
        val = up_conv(d_ff + j * fc)
        a = (g * _sigmoid(g) * val).astype(BF16)
        part = jnp.dot(a, wdn_ref[j * fc:(j + 1) * fc, :], preferred_element_type=F32)
        y = part if y is None else y + part
    o_ref[0] = _layer_norm(alpha * x + g2_ref[0] * y, lg_ref[...], lb_ref[...])


def _ffn(x, sc, sh, g2, wup, cw, cb, wdn, lg, lb, alpha):
    bsz, seq, d = x.shape
    d_ff = wdn.shape[0]
    tm = min(TM_FFN, seq)
    tok = pl.BlockSpec((1, tm, d), lambda b, i: (b, i, 0))
    per_b = pl.BlockSpec((1, 1, d), lambda b, i: (b, 0, 0))
    once = lambda a: pl.BlockSpec(a.shape, lambda b, i: (0,) * a.ndim, pipeline_mode=pl.Buffered(1))
    return pl.pallas_call(
        functools.partial(_ffn_kernel, alpha=alpha, d_ff=d_ff),
        grid=(bsz, seq // tm),
        in_specs=[tok, per_b, per_b, per_b, once(wup), once(cw), once(cb), once(wdn), once(lg), once(lb)],
        out_specs=tok,
        out_shape=jax.ShapeDtypeStruct((bsz, seq, d), F32),
        scratch_shapes=[pltpu.VMEM((8, 2 * d_ff), F32)],
        compiler_params=pltpu.CompilerParams(
            dimension_semantics=("parallel", "arbitrary"), vmem_limit_bytes=VMEM_LIMIT),
        name="conv_ffn_ln",
    )(x, sc, sh, g2, wup, cw, cb, wdn, lg, lb)


def kernel(x, c, positions, rel_bias, w_ada, b_ada, w_in, q_norm_g, w_uq, kv_norm_g, w_ukv,
           w_branch_a, w_branch_b, w_out, ln1_g, ln1_b, w_up, conv_w, conv_b, w_down, ln2_g, ln2_b):
    bsz, seq, d = x.shape
    depth = w_ada.shape[0]
    alpha = (2 * depth) ** 0.25
    topk = min(TOPK_MAX, seq // 4)
    assert seq % TQ == 0 or seq < TQ

    mod = _ada_mod(c, w_ada, b_ada)
    pos3 = positions.reshape(bsz, seq, 1)
    invf = ROPE_THETA ** (-jnp.arange(ROPE_HALF, dtype=F32) * (2.0 / ROPE_DIM))
    invf = jnp.tile(invf, LANES // ROPE_HALF).reshape(1, LANES)
    btab = ((rel_bias[_BUCKET_TABLE] - rel_bias[NUM_BUCKETS - 1][None, :]) * LOG2E).T.astype(F32)
    band = _bias_band(btab)

    for l in range(depth):
        sh1, sc1, g1, sh2, sc2, g2 = [mod[l, :, None, j * d:(j + 1) * d] for j in range(6)]
        qkva, iq, ik, iw, qb, kvb, krb, gate = _inproj(
            x, sc1, sh1, pos3, invf, _pack_w_in(w_in[l], d),
            q_norm_g[l].reshape(1, -1), _pack_w_uq(w_uq[l]),
            kv_norm_g[l].reshape(1, -1), _pack_w_ukv(w_ukv[l]))

        mask = _indexer(iq.transpose(0, 2, 1), iw.transpose(0, 2, 1), ik, topk)
        qaT = qkva[:, :, :A_WIDTH].transpose(0, 2, 1)
        ka = qkva[:, :, A_WIDTH:2 * A_WIDTH]
        vaT = qkva[:, :, 2 * A_WIDTH:].transpose(0, 2, 1)
        yaT = _attention(qaT, ka, vaT, A_HEADS, A_HEAD_DIM, A_HEAD_DIM, mask=mask, pos=positions, btab=btab,
                         band=band)

        n_nope = B_HEADS * NOPE_DIM
        n_r = B_HEADS * ROPE_HALF
        q_heads = jnp.concatenate([
            qb[:, :, :n_nope].reshape(bsz, seq, B_HEADS, NOPE_DIM),
            qb[:, :, n_nope:n_nope + n_r].reshape(bsz, seq, B_HEADS, ROPE_HALF),
            qb[:, :, n_nope + n_r:].reshape(bsz, seq, B_HEADS, ROPE_HALF)], axis=-1)
        qbT = q_heads.reshape(bsz, seq, B_HEADS * B_QK_DIM).transpose(0, 2, 1)
        kr = jnp.concatenate([krb[:, :, :ROPE_HALF], krb[:, :, LANES:LANES + ROPE_HALF]], axis=-1)
        k_heads = jnp.concatenate([
            kvb[:, :, :n_nope].reshape(bsz, seq, B_HEADS, NOPE_DIM),
            jnp.broadcast_to(kr[:, :, None, :], (bsz, seq, B_HEADS, ROPE_DIM))], axis=-1)
        kb = k_heads.reshape(bsz, seq, B_HEADS * B_QK_DIM)
        vbT = kvb[:, :, n_nope:].transpose(0, 2, 1)
        ybT = _attention(qbT, kb, vbT, B_HEADS, B_QK_DIM, V_DIM)

        x = _merge(x, yaT, ybT, gate, g1,
                   w_branch_a[l].astype(BF16), w_branch_b[l].astype(BF16), w_out[l].astype(BF16),
                   ln1_g[l].reshape(1, d), ln1_b[l].reshape(1, d), alpha)
        x = _ffn(x, sc2, sh2, g2, w_up[l].astype(BF16), conv_w[l], conv_b[l].reshape(1, -1),
                 w_down[l].astype(BF16), ln2_g[l].reshape(1, d), ln2_b[l].reshape(1, d), alpha)
    return x
```

```python
import functools
import math

import numpy as np
import jax
import jax.numpy as jnp
from jax import lax
from jax.experimental import pallas as pl
from jax.experimental.pallas import tpu as pltpu

F32 = jnp.float32
BF16 = jnp.bfloat16

A_HEADS = 8
A_HEAD_DIM = 64
IDX_HEADS = 8
IDX_DIM = 64
TOPK_MAX = 256
B_HEADS = 8
Q_RANK = 256
KV_RANK = 128
NOPE_DIM = 64
ROPE_DIM = 32
ROPE_HALF = ROPE_DIM // 2
V_DIM = 64
ROPE_THETA = 10000.0
NUM_BUCKETS = 32
MAX_DISTANCE = 128
LN_EPS = 1e-5
RMS_EPS = 1e-6
A_WIDTH = A_HEADS * A_HEAD_DIM
B_QK_DIM = NOPE_DIM + ROPE_DIM

LANES = 128
V7X_VMEM_BYTES = 64 * 1024 * 1024
VMEM_LIMIT = V7X_VMEM_BYTES * 13 // 16

NEG = -1e30
LOG2E = math.log2(math.e)

TM_PROJ = 512
TM_FFN = 512
FFN_CHUNK = 1408
TQ = 256
TK = 1024
IDX_CHUNK = 256
IDX_SCORE_CHUNK = 256
IDX_SCORE_GROUP = 4
IDX_COUNT_GROUP = 4
ATTN_CHUNK = 128
ONES_ROWS = 16
ATTN_P_BUFFERS = 2
ATTN_SUB, ATTN_S_BUFFERS = 256, 6
ATTN_SUB_MASKED, ATTN_S_BUFFERS_MASKED = 512, 4


def _t5_bucket_table():
    n = np.arange(MAX_DISTANCE)
    max_exact = NUM_BUCKETS // 2
    out = []
    for dt in (np.float32, np.float64):
        ratio = np.log(np.maximum(n, 1).astype(dt) / dt(max_exact)) / dt(math.log(MAX_DISTANCE / max_exact))
        large = max_exact + (ratio * dt(NUM_BUCKETS - max_exact)).astype(np.int32)
        large = np.minimum(large, NUM_BUCKETS - 1)
        out.append(np.where(n < max_exact, n, large))
    assert (out[0] == out[1]).all()
    assert out[0][-1] == NUM_BUCKETS - 1
    return out[0].astype(np.int32)


_BUCKET_TABLE = _t5_bucket_table()


def _ada_kernel(c_ref, w_ref, b_ref, o_ref):
    c = c_ref[...]
    s = c * (1.0 / (1.0 + jnp.exp(-c)))
    o_ref[0] = jnp.dot(s, w_ref[0], preferred_element_type=F32) + b_ref[0]


def _ada_mod(c, w_ada, b_ada):
    depth, d, d6 = w_ada.shape
    bsz = c.shape[0]
    n_chunks = d6 // d
    return pl.pallas_call(
        _ada_kernel,
        grid=(depth, n_chunks),
        in_specs=[
            pl.BlockSpec((bsz, d), lambda l, j: (0, 0)),
            pl.BlockSpec((1, d, d), lambda l, j: (l, 0, j)),
            pl.BlockSpec((1, 1, d), lambda l, j: (l, 0, j)),
        ],
        out_specs=pl.BlockSpec((1, bsz, d), lambda l, j: (l, 0, j)),
        out_shape=jax.ShapeDtypeStruct((depth, bsz, d6), F32),
        name="ada_mod",
    )(c, w_ada, b_ada.reshape(depth, 1, d6))


_C_QKV = 0
_C_IQ = _C_QKV + 3 * A_WIDTH
_C_IK = _C_IQ + IDX_HEADS * IDX_DIM
_C_IW = _C_IK + LANES
_C_CQ = _C_IW + LANES
_C_CKV = _C_CQ + Q_RANK
_C_KR1 = _C_CKV + KV_RANK
_C_KR2 = _C_KR1 + LANES
_C_GATE = _C_KR2 + LANES


def _pack_w_in(w_in, d):
    sizes = (A_WIDTH, A_WIDTH, A_WIDTH, IDX_HEADS * IDX_DIM, IDX_DIM, IDX_HEADS,
             Q_RANK, KV_RANK, ROPE_DIM, d, d)
    offs = np.cumsum((0,) + sizes)
    seg = {name: w_in[:, offs[i]:offs[i + 1]] for i, name in enumerate(
        ("qa", "ka", "va", "iq", "ik", "iw", "cq", "ckv", "kr", "ga", "gb"))}

    def pad(w, width):
        return jnp.pad(w, ((0, 0), (0, width - w.shape[1])))

    parts = [seg["qa"], seg["ka"], seg["va"], seg["iq"],
             pad(seg["ik"], LANES), pad(seg["iw"], LANES),
             seg["cq"], seg["ckv"],
             pad(seg["kr"][:, :ROPE_HALF], LANES), pad(seg["kr"][:, ROPE_HALF:], LANES),
             seg["ga"], seg["gb"]]
    return jnp.concatenate(parts, axis=1).astype(BF16)


def _pack_w_uq(w_uq):
    w = w_uq.reshape(Q_RANK, B_HEADS, B_QK_DIM)
    nope = w[:, :, :NOPE_DIM].reshape(Q_RANK, B_HEADS * NOPE_DIM)
    r1 = w[:, :, NOPE_DIM:NOPE_DIM + ROPE_HALF].reshape(Q_RANK, B_HEADS * ROPE_HALF)
    r2 = w[:, :, NOPE_DIM + ROPE_HALF:].reshape(Q_RANK, B_HEADS * ROPE_HALF)
    return jnp.concatenate([nope, r1, r2], axis=1).astype(BF16)


def _pack_w_ukv(w_ukv):
    w = w_ukv.reshape(KV_RANK, B_HEADS, NOPE_DIM + V_DIM)
    kn = w[:, :, :NOPE_DIM].reshape(KV_RANK, B_HEADS * NOPE_DIM)
    v = w[:, :, NOPE_DIM:].reshape(KV_RANK, B_HEADS * V_DIM)
    return jnp.concatenate([kn, v], axis=1).astype(BF16)


def _sigmoid(x):
    return 1.0 / (1.0 + jnp.exp(-x))


def _rms(x, g):
    return x * lax.rsqrt(jnp.mean(x * x, axis=-1, keepdims=True) + RMS_EPS) * g


def _inproj_kernel(x_ref, sc_ref, sh_ref, pos_ref, invf_ref, w_ref, qg_ref, wuq_ref, kvg_ref, wukv_ref,
                   o_qkva, o_iq, o_ik, o_iw, o_qb, o_kvb, o_krb, o_gate, *, d_model):
    h = (x_ref[0] * (1.0 + sc_ref[0]) + sh_ref[0]).astype(BF16)

    def proj(c0, width):
        return jnp.dot(h, w_ref[:, c0:c0 + width], preferred_element_type=F32)

    a_scale = A_HEAD_DIM ** -0.5 * LOG2E
    o_qkva[0, :, 0:A_WIDTH] = (proj(_C_QKV, A_WIDTH) * a_scale).astype(BF16)
    o_qkva[0, :, A_WIDTH:3 * A_WIDTH] = proj(_C_QKV + A_WIDTH, 2 * A_WIDTH).astype(BF16)
    o_iq[0] = proj(_C_IQ, IDX_HEADS * IDX_DIM).astype(BF16)
    o_ik[0] = proj(_C_IK, LANES)[:, :IDX_DIM].astype(BF16)
    o_iw[0] = proj(_C_IW, LANES)[:, :IDX_HEADS] * (IDX_DIM ** -0.5 * IDX_HEADS ** -0.5)

    ang = pos_ref[0].astype(F32) * invf_ref[...]
    cos, sin = jnp.cos(ang), jnp.sin(ang)

    b_scale = B_QK_DIM ** -0.5 * LOG2E
    cq = _rms(proj(_C_CQ, Q_RANK), qg_ref[...]).astype(BF16)
    q = jnp.dot(cq, wuq_ref[...], preferred_element_type=F32)
    n_nope = B_HEADS * NOPE_DIM
    n_r = B_HEADS * ROPE_HALF
    x1 = q[:, n_nope:n_nope + n_r]
    x2 = q[:, n_nope + n_r:]
    o_qb[0, :, 0:n_nope] = (q[:, :n_nope] * b_scale).astype(BF16)
    o_qb[0, :, n_nope:n_nope + n_r] = ((x1 * cos - x2 * sin) * b_scale).astype(BF16)
    o_qb[0, :, n_nope + n_r:] = ((x1 * sin + x2 * cos) * b_scale).astype(BF16)

    ckv = _rms(proj(_C_CKV, KV_RANK), kvg_ref[...]).astype(BF16)
    o_kvb[0] = jnp.dot(ckv, wukv_ref[...], preferred_element_type=F32).astype(BF16)

    k1 = proj(_C_KR1, LANES)
    k2 = proj(_C_KR2, LANES)
    o_krb[0, :, 0:LANES] = (k1 * cos - k2 * sin).astype(BF16)
    o_krb[0, :, LANES:] = (k1 * sin + k2 * cos).astype(BF16)

    o_gate[0] = _sigmoid(proj(_C_GATE, 2 * d_model)).astype(BF16)


def _inproj(x, sc, sh, pos3, invf, w_pack, qg, wuq, kvg, wukv):
    bsz, seq, d = x.shape
    tm = min(TM_PROJ, seq)
    grid = (bsz, seq // tm)
    tok = lambda width: pl.BlockSpec((1, tm, width), lambda b, i: (b, i, 0))
    per_b = pl.BlockSpec((1, 1, d), lambda b, i: (b, 0, 0))
    full = lambda a: pl.BlockSpec(a.shape, lambda b, i: (0,) * a.ndim)
    out_widths = (3 * A_WIDTH, IDX_HEADS * IDX_DIM, IDX_DIM, IDX_HEADS,
                  B_HEADS * B_QK_DIM, B_HEADS * (NOPE_DIM + V_DIM), 2 * LANES, 2 * d)
    out_dtypes = (BF16, BF16, BF16, F32, BF16, BF16, BF16, BF16)
    return pl.pallas_call(
        functools.partial(_inproj_kernel, d_model=d),
        grid=grid,
        in_specs=[tok(d), per_b, per_b, tok(1), full(invf), full(w_pack), full(qg), full(wuq),
                  full(kvg), full(wukv)],
        out_specs=[tok(w) for w in out_widths],
        out_shape=[jax.ShapeDtypeStruct((bsz, seq, w), dt) for w, dt in zip(out_widths, out_dtypes)],
        compiler_params=pltpu.CompilerParams(
            dimension_semantics=("parallel", "parallel"), vmem_limit_bytes=VMEM_LIMIT),
        name="in_proj",
    )(x, sc, sh, pos3, invf, w_pack, qg, wuq, kvg, wukv)


def _ordered_key(x):
    b = lax.bitcast_convert_type(x, jnp.int32)
    return jnp.where(b < 0, b ^ jnp.int32(0x7FFFFFFF), b)


def _from_ordered_key(k):
    b = jnp.where(k < 0, k ^ jnp.int32(0x7FFFFFFF), k)
    return lax.bitcast_convert_type(b, F32)


def _indexer_kernel(iqT_ref, iwT_ref, ik_ref, o_ref, tcnt_ref, *, topk, seq):
    tq = o_ref.shape[2]
    ch = IDX_CHUNK
    sch = min(IDX_SCORE_CHUNK, ch)
    qi = pl.program_id(1)
    n_chunks = (qi * tq) // ch + tq // ch
    sub = 8
    kf = float(topk)

    def fold(v, op):
        return op(v.reshape(v.shape[0] // sub, sub, tq), axis=0)

    def count(pred):
        return fold(jnp.where(pred, 1.0, 0.0), jnp.sum)

    def chunk(c):
        k0 = pl.multiple_of(c * ch, ch)
        return k0, o_ref[0, pl.ds(k0, ch), :]

    def score_chunk(c, carry, diagonal):
        vmax, vmin, n_pos, n_nonneg = carry
        k0 = pl.multiple_of(c * sch, sch)
        ik = ik_ref[0, pl.ds(k0, sch), :]
        acc = jnp.zeros((sch, tq), F32)
        for hd in range(IDX_HEADS):
            y = jnp.dot(ik, iqT_ref[0, hd * IDX_DIM:(hd + 1) * IDX_DIM, :],
                        preferred_element_type=F32)
            acc = acc + jnp.maximum(y, 0.0) * iwT_ref[0, hd:hd + 1, :]
        if diagonal:
            key_idx = k0 + lax.broadcasted_iota(jnp.int32, (sch, tq), 0)
            q_idx = qi * tq + lax.broadcasted_iota(jnp.int32, (sch, tq), 1)
            causal = key_idx <= q_idx
            acc = jnp.where(causal, acc, -jnp.inf)
            vmin = jnp.minimum(vmin, fold(jnp.where(causal, acc, jnp.inf), jnp.min))
        else:
            vmin = jnp.minimum(vmin, fold(acc, jnp.min))
        o_ref[0, pl.ds(k0, sch), :] = acc
        vmax = jnp.maximum(vmax, fold(acc, jnp.max))
        n_pos = n_pos + count(acc > 0.0)
        n_nonneg = n_nonneg + count(acc >= 0.0)
        return vmax, vmin, n_pos, n_nonneg

    n_score = n_chunks * (ch // sch)
    n_diag = tq // sch
    n_full = n_score - n_diag
    sg = IDX_SCORE_GROUP

    def score_group(g, carry):
        for i in range(sg):
            carry = score_chunk(g * sg + i, carry, diagonal=False)
        return carry

    stats = (jnp.full((sub, tq), -jnp.inf, F32), jnp.full((sub, tq), jnp.inf, F32),
             jnp.zeros((sub, tq), F32), jnp.zeros((sub, tq), F32))
    stats = lax.fori_loop(0, n_full // sg, score_group, stats)
    stats = lax.fori_loop((n_full // sg) * sg, n_full, functools.partial(score_chunk, diagonal=False), stats)
    stats = lax.fori_loop(n_full, n_score, functools.partial(score_chunk, diagonal=True), stats)
    row_max = jnp.max(stats[0], axis=0, keepdims=True)
    row_min = jnp.min(stats[1], axis=0, keepdims=True)
    n_pos = jnp.sum(stats[2], axis=0, keepdims=True)
    n_nonneg = jnp.sum(stats[3], axis=0, keepdims=True)

    group = math.gcd(seq // ch, IDX_COUNT_GROUP)
    n_steps = (n_chunks + group - 1) // group

    def pad_chunk(c, carry):
        o_ref[0, pl.ds(pl.multiple_of(c * ch, ch), ch), :] = jnp.full((ch, tq), -jnp.inf, F32)
        return carry
    lax.fori_loop(n_chunks, n_steps * group, pad_chunk, 0)

    def count_ge(thr):
        def body(c, parts):
            return tuple(p + count(chunk(c * group + i)[1] >= thr) for i, p in enumerate(parts))
        parts = lax.fori_loop(0, n_steps, body, (jnp.zeros((sub, tq), F32),) * group)
        return jnp.sum(functools.reduce(jnp.add, parts), axis=0, keepdims=True)

    n_valid = (qi * tq + lax.broadcasted_iota(jnp.int32, (1, tq), 1) + 1).astype(F32)

    keep_all = n_valid <= kf
    zero_tie = jnp.logical_not(keep_all) & (n_pos < kf) & (n_nonneg >= kf)
    positive = n_pos >= kf
    lo_key0 = jnp.where(positive, 1, _ordered_key(row_min))
    hi_key0 = jnp.where(positive, _ordered_key(row_max) + 1, -2)
    n_hi0 = jnp.where(positive, 0.0, n_nonneg)
    settled = keep_all | zero_tie
    short0 = jnp.logical_not(settled) & (n_hi0 == kf - 1.0)
    done0 = jnp.where(settled | short0, 1, 0).astype(jnp.int32)
    thr0 = jnp.where(keep_all, -jnp.inf, jnp.where(zero_tie, 0.0, row_min))
    tie0 = jnp.where(zero_tie, 1, 0).astype(jnp.int32)

    def n_active(done):
        return jnp.sum(jnp.where(done > 0, 0.0, 1.0))

    def cond(st):
        return st[0] > 0.0

    def body(st):
        _, it, lo_key, hi_key, n_hi, thr, done, tie, short = st
        lo = _from_ordered_key(lo_key)
        hi = _from_ordered_key(hi_key)
        mid_val_key = _ordered_key(lo * 0.5 + hi * 0.5)
        mid_bit_key = (lo_key & hi_key) + ((lo_key ^ hi_key) >> 1)
        use_val = (it < 4) & (jnp.abs(hi) < jnp.inf) & (jnp.abs(lo) < jnp.inf)
        mid_key = jnp.where(use_val, mid_val_key, mid_bit_key)
        mid_key = jnp.minimum(jnp.maximum(mid_key, lo_key + 1), hi_key - 1)
        mid = _from_ordered_key(mid_key)
        cnt = count_ge(mid)
        active = done == 0
        ge = cnt >= kf
        lo_key = jnp.where(active & ge, mid_key, lo_key)
        hi_key = jnp.where(active & jnp.logical_not(ge), mid_key, hi_key)
        n_hi = jnp.where(active & jnp.logical_not(ge), cnt, n_hi)
        hit = active & (cnt == kf)
        stuck = active & jnp.logical_not(hit) & (hi_key - 1 <= lo_key)
        one_short = active & jnp.logical_not(hit | stuck) & (n_hi == kf - 1.0)
        thr = jnp.where(hit, mid, jnp.where(stuck, _from_ordered_key(lo_key), thr))
        tie = jnp.where(stuck, 1, tie)
        short = jnp.where(one_short, 1, short)
        done = jnp.where(hit | stuck | one_short, 1, done)
        return n_active(done), it + 1, lo_key, hi_key, n_hi, thr, done, tie, short

    st = lax.while_loop(cond, body, (n_active(done0), jnp.int32(0), lo_key0, hi_key0, n_hi0, thr0, done0, tie0,
                                     jnp.where(short0, 1, 0).astype(jnp.int32)))
    thr, tie, short = st[5], st[7] > 0, st[8] > 0
    hi_end = _from_ordered_key(st[3])

    def max_below_hi():
        def body(c, parts):
            return tuple(jnp.maximum(p, fold(jnp.where(v < hi_end, v, -jnp.inf), jnp.max))
                         for p, v in ((p, chunk(c * group + i)[1]) for i, p in enumerate(parts)))
        parts = lax.fori_loop(0, n_steps, body, (jnp.full((sub, tq), -jnp.inf, F32),) * group)
        return jnp.max(functools.reduce(jnp.maximum, parts), axis=0, keepdims=True)

    n_short = jnp.sum(jnp.where(short, 1.0, 0.0))
    thr = jnp.where(short, lax.cond(n_short > 0.0, max_below_hi, lambda: thr), thr)
    tie = tie | short
    n_tie = jnp.sum(jnp.where(tie, 1.0, 0.0))

    def write_mask(k0, sel):
        o_ref[0, pl.ds(k0, ch), :] = jnp.where(sel, 0.0, NEG)

    def grouped(per_chunk):
        def body(g, carry):
            for i in range(group):
                carry = per_chunk(g * group + i, carry)
            return carry
        return body

    f32_info = jnp.finfo(F32)
    cut_incl = jnp.where(thr == -jnp.inf, float(f32_info.min), thr)
    above = _from_ordered_key(_ordered_key(thr) + 1)
    cut_excl = jnp.where((above >= 0.0) & (above < float(f32_info.tiny)), float(f32_info.tiny), above)

    def mask_plain():
        def body(c, carry):
            k0, v = chunk(c)
            write_mask(k0, v >= cut_incl)
            return carry
        lax.fori_loop(0, n_steps, grouped(body), 0)

    def mask_ties():
        def count_body(c, n_gt):
            _, v = chunk(c)
            tcnt_ref[c] = count(v == thr)
            return n_gt + count(v > thr)
        n_gt = lax.fori_loop(0, n_steps, grouped(count_body), jnp.zeros((sub, tq), F32))
        n_gt = jnp.sum(n_gt, axis=0, keepdims=True)
        need = jnp.where(tie, kf - n_gt, float(seq))

        def quota_body(c, carry):
            before, split = carry
            t = jnp.sum(tcnt_ref[c], axis=0, keepdims=True)
            keep = jnp.clip(need - before, 0.0, t)
            tcnt_ref[c, 0:1, :] = keep
            split = jnp.where((keep > 0.0) & (keep < t), c.astype(F32), split)
            return before + t, split
        _, split0 = lax.fori_loop(0, n_steps * group, quota_body,
                                  (jnp.zeros((1, tq), F32), jnp.full((1, tq), -1.0, F32)))

        def split_body(st):
            c_f, split = st
            c = c_f.astype(jnp.int32)
            k0, v = chunk(c)
            is_tie = v == thr
            r = lax.broadcasted_iota(jnp.int32, (ch, ch), 0)
            s = lax.broadcasted_iota(jnp.int32, (ch, ch), 1)
            lower = jnp.where(s < r, 1.0, 0.0).astype(BF16)
            rank = jnp.dot(lower, jnp.where(is_tie, 1.0, 0.0).astype(BF16), preferred_element_type=F32)
            o_ref[0, pl.ds(k0, ch), :] = jnp.where(is_tie & (rank >= tcnt_ref[c, 0:1, :]), -jnp.inf, v)
            split = jnp.where(split == c_f, -1.0, split)
            return jnp.max(split), split
        lax.while_loop(lambda st: st[0] >= 0.0, split_body, (jnp.max(split0), split0))

        def body(c, carry):
            k0, v = chunk(c)
            keep_ties = tcnt_ref[c, 0:1, :] > 0.0
            write_mask(k0, v >= jnp.where(keep_ties, cut_incl, cut_excl))
            return carry
        lax.fori_loop(0, n_steps, grouped(body), 0)

    lax.cond(n_tie > 0.0, mask_ties, mask_plain)

    def fill_chunk(c, carry):
        k0 = pl.multiple_of(c * ch, ch)
        o_ref[0, pl.ds(k0, ch), :] = jnp.full((ch, tq), NEG, F32)
        return carry

    lax.fori_loop(n_chunks, seq // ch, fill_chunk, 0)


def _indexer(iqT, iwT, ik, topk):
    bsz, _, seq = iqT.shape
    tq = min(TQ, seq)
    return pl.pallas_call(
        functools.partial(_indexer_kernel, topk=topk, seq=seq),
        grid=(bsz, seq // tq),
        in_specs=[
            pl.BlockSpec((1, IDX_HEADS * IDX_DIM, tq), lambda b, i: (b, 0, i)),
            pl.BlockSpec((1, IDX_HEADS, tq), lambda b, i: (b, 0, i)),
            pl.BlockSpec((1, seq, IDX_DIM), lambda b, i: (b, 0, 0)),
        ],
        out_specs=pl.BlockSpec((1, seq, tq), lambda b, i: (b, 0, i)),
        out_shape=jax.ShapeDtypeStruct((bsz, seq, seq), F32),
        scratch_shapes=[pltpu.VMEM((max(seq // IDX_CHUNK, 1), 8, tq), F32)],
        compiler_params=pltpu.CompilerParams(
            dimension_semantics=("parallel", "parallel"), vmem_limit_bytes=VMEM_LIMIT),
        name="dsa_indexer",
    )(iqT, iwT, ik)


def _attn_kernel(*refs, n_heads, dk, dv, use_mask):
    if use_mask:
        (pq_ref, pk_ref, qmin_ref, kmax_ref, reg_ref, qT_ref, k_ref, vT_ref, mask_ref, qpos_ref, kpos_ref,
         btab_ref, band_ref, o_ref, m_sc, acc_sc, s_buf, p_buf) = refs
    else:
        pq_ref, pk_ref, qT_ref, k_ref, vT_ref, o_ref, m_sc, acc_sc, s_buf, p_buf = refs
    tq = qT_ref.shape[2]
    tk = k_ref.shape[1]
    b = pl.program_id(0)
    pair = pl.program_id(1)
    qi = pq_ref[pair]
    ki = pk_ref[pair]
    last_k = ((qi + 1) * tq - 1) // tk

    @pl.when(ki == 0)
    def _init():
        m_sc[...] = jnp.full(m_sc.shape, NEG, F32)
        acc_sc[...] = jnp.zeros(acc_sc.shape, F32)

    ts = s_buf.shape[1]
    ck = min(ATTN_CHUNK, ts)
    n_chunks = ts // ck
    dva = vT_ref.shape[1] // n_heads
    rows = lambda g: slice(g * ck, (g + 1) * ck)

    n_sub = tk // ts
    n_live = jnp.minimum(((qi + 1) * tq - ki * tk + ts - 1) // ts, n_sub)

    def step(extra, live=n_sub):
        n_buf = s_buf.shape[0]
        n_pbuf = p_buf.shape[0]
        ahead = n_buf - 1
        units = [(j, h) for j in range(live) for h in range(n_heads)]

        def logits_to_buf(i):
            j, h = units[i]
            part = None
            for c in range(n_chunks):
                g = j * n_chunks + c
                s = jnp.dot(k_ref[0, rows(g), h * dk:(h + 1) * dk], qT_ref[0, h * dk:(h + 1) * dk, :],
                            preferred_element_type=F32)
                s = extra(h, g, s)
                s_buf[i % n_buf, c * ck:(c + 1) * ck, :] = s
                part = s if part is None else jnp.maximum(part, s)
            return jnp.max(part, axis=0, keepdims=True)

        tile_max = {i: logits_to_buf(i) for i in range(min(ahead, len(units)))}
        for i, (j, h) in enumerate(units):
            m_prev = m_sc[h:h + 1, :]
            m_new = jnp.maximum(m_prev, tile_max.pop(i))
            alpha = jnp.exp2(m_prev - m_new)
            if i + ahead < len(units):
                tile_max[i + ahead] = logits_to_buf(i + ahead)
            p_buf[i % n_pbuf] = jnp.exp2(s_buf[i % n_buf] - m_new).astype(BF16)
            pv = jnp.dot(vT_ref[0, h * dva:(h + 1) * dva, j * ts:(j + 1) * ts], p_buf[i % n_pbuf],
                         preferred_element_type=F32)
            acc_sc[h * dva:(h + 1) * dva, :] = acc_sc[h * dva:(h + 1) * dva, :] * alpha + pv
            m_sc[h:h + 1, :] = m_new

    if use_mask:
        near = qmin_ref[b, qi] - kmax_ref[b, ki] < MAX_DISTANCE

        @pl.when(jnp.logical_not(near))
        def _far():
            step(lambda h, c, s: s + mask_ref[0, rows(c), :])

        consecutive = reg_ref[b] > 0

        def banded(h, c, s):
            parts = []
            for j in range(tq // LANES):
                dist = (qi * tq + j * LANES) - (ki * tk + c * ck)
                parts.append(jnp.where(dist == 0, band_ref[h, 0],
                                       jnp.where(dist == ck, band_ref[h, 1], 0.0)))
            return s + mask_ref[0, rows(c), :] + jnp.concatenate(parts, axis=1)

        for live in range(1, n_sub + 1):
            pl.when(near & consecutive & (n_live == live))(functools.partial(step, banded, live))

        @pl.when(near & jnp.logical_not(consecutive))
        def _near_general():
            def extra(h, c, s):
                rel = qpos_ref[0] - kpos_ref[0, rows(c), :]
                n = jnp.clip(rel, 0, MAX_DISTANCE - 1)
                tab = jnp.broadcast_to(btab_ref[h:h + 1, :], (ck, LANES))
                parts = [jnp.take_along_axis(tab, n[:, j * LANES:(j + 1) * LANES], axis=1)
                         for j in range(tq // LANES)]
                return s + mask_ref[0, rows(c), :] + jnp.concatenate(parts, axis=1)
            step(extra)
    else:
        diag = (ki + 1) * tk > qi * tq

        @pl.when(jnp.logical_not(diag))
        def _full():
            step(lambda h, c, s: s)

        def causal(h, c, s):
            key_idx = ki * tk + c * ck + lax.broadcasted_iota(jnp.int32, (ck, tq), 0)
            q_idx = qi * tq + lax.broadcasted_iota(jnp.int32, (ck, tq), 1)
            return jnp.where(key_idx <= q_idx, s, NEG)

        for live in range(1, n_sub + 1):
            pl.when(diag & (n_live == live))(functools.partial(step, causal, live))

    @pl.when(ki == last_k)
    def _fin():
        for h in range(n_heads):
            inv = 1.0 / acc_sc[h * dva + dv:h * dva + dv + 1, :]
            o_ref[0, h * dv:(h + 1) * dv, :] = (acc_sc[h * dva:h * dva + dv, :] * inv).astype(o_ref.dtype)


def _causal_pairs(nq, tq, tk):
    pq, pk = [], []
    for qi in range(nq):
        for ki in range(((qi + 1) * tq - 1) // tk + 1):
            pq.append(qi)
            pk.append(ki)
    return np.asarray(pq, np.int32), np.asarray(pk, np.int32)


def _bias_band(btab):
    assert MAX_DISTANCE == LANES
    n_h, n = btab.shape

    def toeplitz(g):
        tiled = jnp.broadcast_to(jnp.pad(g, ((0, 0), (0, 1)))[:, None, :], (n_h, n, 2 * n))
        skew = tiled.reshape(n_h, 2 * n * n)[:, :n * (2 * n - 1)].reshape(n_h, n, 2 * n - 1)
        return skew[:, :, n - 1:]

    same = jnp.concatenate([jnp.broadcast_to(btab[:, :1], (n_h, n - 1)), btab], axis=1)
    apart = jnp.concatenate([btab[:, 1:], jnp.broadcast_to(btab[:, n - 1:], (n_h, n))], axis=1)
    return jnp.stack([toeplitz(same), toeplitz(apart)], axis=1)


def _attention(qT, k, vT, n_heads, dk, dv, mask=None, pos=None, btab=None, band=None):
    bsz, _, seq = qT.shape
    dva = dv + ONES_ROWS
    vT = jnp.concatenate([vT.reshape(bsz, n_heads, dv, seq),
                          jnp.ones((bsz, n_heads, ONES_ROWS, seq), vT.dtype)], axis=2)
    vT = vT.reshape(bsz, n_heads * dva, seq)
    tq = min(TQ, seq)
    tk = min(TK, seq)
    use_mask = mask is not None
    sub_keys, s_buffers = (ATTN_SUB_MASKED, ATTN_S_BUFFERS_MASKED) if use_mask else (ATTN_SUB, ATTN_S_BUFFERS)
    ts = min(sub_keys, tk)
    nq, nk = seq // tq, seq // tk
    pq, pk = _causal_pairs(nq, tq, tk)

    in_specs = [
        pl.BlockSpec((1, n_heads * dk, tq), lambda b, s, pq, pk, *_: (b, 0, pq[s])),
        pl.BlockSpec((1, tk, n_heads * dk), lambda b, s, pq, pk, *_: (b, pk[s], 0)),
        pl.BlockSpec((1, n_heads * dva, tk), lambda b, s, pq, pk, *_: (b, 0, pk[s])),
    ]
    args = [qT, k, vT]
    prefetch = [jnp.asarray(pq), jnp.asarray(pk)]
    if use_mask:
        assert min(ATTN_CHUNK, ts) == LANES
        consecutive = jnp.all(pos[:, 1:] - pos[:, :-1] == 1, axis=1).astype(jnp.int32)
        prefetch += [jnp.min(pos.reshape(bsz, nq, tq), axis=-1), jnp.max(pos.reshape(bsz, nk, tk), axis=-1),
                     consecutive]
        in_specs += [
            pl.BlockSpec((1, tk, tq), lambda b, s, pq, pk, *_: (b, pk[s], pq[s])),
            pl.BlockSpec((1, 1, tq), lambda b, s, pq, pk, *_: (b, 0, pq[s])),
            pl.BlockSpec((1, tk, 1), lambda b, s, pq, pk, *_: (b, pk[s], 0)),
            pl.BlockSpec(btab.shape, lambda b, s, pq, pk, *_: (0, 0)),
            pl.BlockSpec(band.shape, lambda b, s, pq, pk, *_: (0, 0, 0, 0)),
        ]
        args += [mask, pos.reshape(bsz, 1, seq), pos.reshape(bsz, seq, 1), btab, band]
    grid_spec = pltpu.PrefetchScalarGridSpec(
        num_scalar_prefetch=len(prefetch),
        grid=(bsz, len(pq)),
        in_specs=in_specs,
        out_specs=pl.BlockSpec((1, n_heads * dv, tq), lambda b, s, pq, pk, *_: (b, 0, pq[s])),
        scratch_shapes=[pltpu.VMEM((n_heads, tq), F32), pltpu.VMEM((n_heads * dva, tq), F32),
                        pltpu.VMEM((s_buffers, ts, tq), F32), pltpu.VMEM((ATTN_P_BUFFERS, ts, tq), BF16)],
    )
    return pl.pallas_call(
        functools.partial(_attn_kernel, n_heads=n_heads, dk=dk, dv=dv, use_mask=use_mask),
        grid_spec=grid_spec,
        out_shape=jax.ShapeDtypeStruct((bsz, n_heads * dv, seq), BF16),
        compiler_params=pltpu.CompilerParams(dimension_semantics=("parallel", "arbitrary"),
                                             vmem_limit_bytes=VMEM_LIMIT),
        name="dsa_attention" if use_mask else "mla_attention",
    )(*prefetch, *args)


def _layer_norm(z, g, b):
    mu = jnp.mean(z, axis=-1, keepdims=True)
    zc = z - mu
    var = jnp.mean(zc * zc, axis=-1, keepdims=True)
    return zc * lax.rsqrt(var + LN_EPS) * g + b


def _merge_kernel(x_ref, yaT_ref, ybT_ref, gate_ref, g1_ref, wa_ref, wb_ref, wo_ref, lg_ref, lb_ref,
                  o_ref, *, alpha):
    d = x_ref.shape[2]
    tn = (((0,), (0,)), ((), ()))
    pa = lax.dot_general(yaT_ref[0], wa_ref[...], tn, preferred_element_type=F32)
    pb = lax.dot_general(ybT_ref[0], wb_ref[...], tn, preferred_element_type=F32)
    gate = gate_ref[0].astype(F32)
    merged = gate[:, :d] * pa + gate[:, d:] * pb
    y = jnp.dot(merged.astype(BF16), wo_ref[...], preferred_element_type=F32)
    o_ref[0] = _layer_norm(alpha * x_ref[0] + g1_ref[0] * y, lg_ref[...], lb_ref[...])


def _merge(x, yaT, ybT, gate, g1, wa, wb, wo, lg, lb, alpha):
    bsz, seq, d = x.shape
    tm = min(TM_PROJ, seq)
    tok = lambda width: pl.BlockSpec((1, tm, width), lambda b, i: (b, i, 0))
    tokT = lambda width: pl.BlockSpec((1, width, tm), lambda b, i: (b, 0, i))
    full = lambda a: pl.BlockSpec(a.shape, lambda b, i: (0,) * a.ndim)
    return pl.pallas_call(
        functools.partial(_merge_kernel, alpha=alpha),
        grid=(bsz, seq // tm),
        in_specs=[tok(d), tokT(yaT.shape[1]), tokT(ybT.shape[1]), tok(2 * d),
                  pl.BlockSpec((1, 1, d), lambda b, i: (b, 0, 0)),
                  full(wa), full(wb), full(wo), full(lg), full(lb)],
        out_specs=tok(d),
        out_shape=jax.ShapeDtypeStruct((bsz, seq, d), F32),
        compiler_params=pltpu.CompilerParams(
            dimension_semantics=("parallel", "parallel"), vmem_limit_bytes=VMEM_LIMIT),
        name="merge_out_ln",
    )(x, yaT, ybT, gate, g1, wa, wb, wo, lg, lb)


def _ffn_kernel(x_ref, sc_ref, sh_ref, g2_ref, wup_ref, cw_ref, cb_ref, wdn_ref, lg_ref, lb_ref,
                o_ref, hist_ref, *, alpha, d_ff):
    i = pl.program_id(1)
    tm = x_ref.shape[1]
    x = x_ref[0]
    h = (x * (1.0 + sc_ref[0]) + sh_ref[0]).astype(BF16)

    @pl.when(i == 0)
    def _():
        hist_ref[...] = jnp.zeros(hist_ref.shape, F32)

    fc = math.gcd(d_ff, FFN_CHUNK)
    row = lax.broadcasted_iota(jnp.int32, (tm, fc), 0)

    def up_conv(c0):
        cols = slice(c0, c0 + fc)
        u = jnp.dot(h, wup_ref[:, cols], preferred_element_type=F32)
        prev = hist_ref[:, cols]
        u1 = jnp.where(row == 0, prev[7:8, :], pltpu.roll(u, 1, axis=0))
        u2 = jnp.where(row == 0, prev[6:7, :], jnp.where(row == 1, prev[7:8, :], pltpu.roll(u, 2, axis=0)))
        hist_ref[:, cols] = u[tm - 8:, :]
        return cw_ref[0:1, cols] * u2 + cw_ref[1:2, cols] * u1 + cw_ref[2:3, cols] * u + cb_ref[:, cols]

    y = None
    for j in range(d_ff // fc):
        g = up_conv(j * fc)
        val = up_conv(d_ff + j * fc)
        a = (g * _sigmoid(g) * val).astype(BF16)
        part = jnp.dot(a, wdn_ref[j * fc:(j + 1) * fc, :], preferred_element_type=F32)
        y = part if y is None else y + part
    o_ref[0] = _layer_norm(alpha * x + g2_ref[0] * y, lg_ref[...], lb_ref[...])


def _ffn(x, sc, sh, g2, wup, cw, cb, wdn, lg, lb, alpha):
    bsz, seq, d = x.shape
    d_ff = wdn.shape[0]
    tm = min(TM_FFN, seq)
    tok = pl.BlockSpec((1, tm, d), lambda b, i: (b, i, 0))
    per_b = pl.BlockSpec((1, 1, d), lambda b, i: (b, 0, 0))
    once = lambda a: pl.BlockSpec(a.shape, lambda b, i: (0,) * a.ndim, pipeline_mode=pl.Buffered(1))
    return pl.pallas_call(
        functools.partial(_ffn_kernel, alpha=alpha, d_ff=d_ff),
        grid=(bsz, seq // tm),
        in_specs=[tok, per_b, per_b, per_b, once(wup), once(cw), once(cb), once(wdn), once(lg), once(lb)],
        out_specs=tok,
        out_shape=jax.ShapeDtypeStruct((bsz, seq, d), F32),
        scratch_shapes=[pltpu.VMEM((8, 2 * d_ff), F32)],
        compiler_params=pltpu.CompilerParams(
            dimension_semantics=("parallel", "arbitrary"), vmem_limit_bytes=VMEM_LIMIT),
        name="conv_ffn_ln",
    )(x, sc, sh, g2, wup, cw, cb, wdn, lg, lb)


def kernel(x, c, positions, rel_bias, w_ada, b_ada, w_in, q_norm_g, w_uq, kv_norm_g, w_ukv,
           w_branch_a, w_branch_b, w_out, ln1_g, ln1_b, w_up, conv_w, conv_b, w_down, ln2_g, ln2_b):
    bsz, seq, d = x.shape
    depth = w_ada.shape[0]
    alpha = (2 * depth) ** 0.25
    topk = min(TOPK_MAX, seq // 4)
    assert seq % TQ == 0 or seq < TQ

    mod = _ada_mod(c, w_ada, b_ada)
    pos3 = positions.reshape(bsz, seq, 1)
    invf = ROPE_THETA ** (-jnp.arange(ROPE_HALF, dtype=F32) * (2.0 / ROPE_DIM))
    invf = jnp.tile(invf, LANES // ROPE_HALF).reshape(1, LANES)
    btab = ((rel_bias[_BUCKET_TABLE] - rel_bias[NUM_BUCKETS - 1][None, :]) * LOG2E).T.astype(F32)
    band = _bias_band(btab)

    for l in range(depth):
        sh1, sc1, g1, sh2, sc2, g2 = [mod[l, :, None, j * d:(j + 1) * d] for j in range(6)]
        qkva, iq, ik, iw, qb, kvb, krb, gate = _inproj(
            x, sc1, sh1, pos3, invf, _pack_w_in(w_in[l], d),
            q_norm_g[l].reshape(1, -1), _pack_w_uq(w_uq[l]),
            kv_norm_g[l].reshape(1, -1), _pack_w_ukv(w_ukv[l]))

        mask = _indexer(iq.transpose(0, 2, 1), iw.transpose(0, 2, 1), ik, topk)
        qaT = qkva[:, :, :A_WIDTH].transpose(0, 2, 1)
        ka = qkva[:, :, A_WIDTH:2 * A_WIDTH]
        vaT = qkva[:, :, 2 * A_WIDTH:].transpose(0, 2, 1)
        yaT = _attention(qaT, ka, vaT, A_HEADS, A_HEAD_DIM, A_HEAD_DIM, mask=mask, pos=positions, btab=btab,
                         band=band)

        n_nope = B_HEADS * NOPE_DIM
        n_r = B_HEADS * ROPE_HALF
        q_heads = jnp.concatenate([
            qb[:, :, :n_nope].reshape(bsz, seq, B_HEADS, NOPE_DIM),
            qb[:, :, n_nope:n_nope + n_r].reshape(bsz, seq, B_HEADS, ROPE_HALF),
            qb[:, :, n_nope + n_r:].reshape(bsz, seq, B_HEADS, ROPE_HALF)], axis=-1)
        qbT = q_heads.reshape(bsz, seq, B_HEADS * B_QK_DIM).transpose(0, 2, 1)
        kr = jnp.concatenate([krb[:, :, :ROPE_HALF], krb[:, :, LANES:LANES + ROPE_HALF]], axis=-1)
        k_heads = jnp.concatenate([
            kvb[:, :, :n_nope].reshape(bsz, seq, B_HEADS, NOPE_DIM),
            jnp.broadcast_to(kr[:, :, None, :], (bsz, seq, B_HEADS, ROPE_DIM))], axis=-1)
        kb = k_heads.reshape(bsz, seq, B_HEADS * B_QK_DIM)
        vbT = kvb[:, :, n_nope:].transpose(0, 2, 1)
        ybT = _attention(qbT, kb, vbT, B_HEADS, B_QK_DIM, V_DIM)

        x = _merge(x, yaT, ybT, gate, g1,
                   w_branch_a[l].astype(BF16), w_branch_b[l].astype(BF16), w_out[l].astype(BF16),
                   ln1_g[l].reshape(1, d), ln1_b[l].reshape(1, d), alpha)
        x = _ffn(x, sc2, sh2, g2, w_up[l].astype(BF16), conv_w[l], conv_b[l].reshape(1, -1),
                 w_down[l].astype(BF16), ln2_g[l].reshape(1, d), ln2_b[l].reshape(1, d), alpha)
    return x
```
